```python
import math
import jax, jax.numpy as jnp
from jax import lax
import numpy as np

D_MODEL = 2048
BATCH = 16
SEQ = 2048
DEPTH = 2
DEC_BATCH = 4
DEC_SEQ = 4096
PAST_LEN = 128

HEAD_DIM = 128
A_HEADS = D_MODEL // 256
A_KV_HEADS = A_HEADS // 4
A_WINDOW = 128
A_BLOCK = 128
B_HEADS = D_MODEL // 256
GRID_W = 64
NB_ROWS_MAX = 8
NB_COLS = 16
D_FF = 4 * D_MODEL
EPS = 1e-6
A_Q = A_HEADS * HEAD_DIM
A_KV = A_KV_HEADS * HEAD_DIM
B_W = B_HEADS * HEAD_DIM
IN_COLS = A_Q + 2 * A_KV + 3 * B_W + 2 * D_MODEL
NEG = -1e30

kernel_name = "hybrid_window_gqa_neighbourhood_encoder"


def _rms(x, g):
    xf = x.astype(jnp.float32)
    y = xf * lax.rsqrt(jnp.mean(xf * xf, axis=-1, keepdims=True) + EPS)
    return (y * g.astype(jnp.float32)).astype(x.dtype)


def _alibi_slopes(n_heads):
    return 2.0 ** (-8.0 * jnp.arange(1, n_heads + 1, dtype=jnp.float32) / n_heads)


def _window_gqa(q, k, v, sink):
    B, S, Hq, d = q.shape
    Hkv = k.shape[2]
    G = Hq // Hkv
    nb = S // A_BLOCK
    pad = ((0, 0), (A_BLOCK, A_BLOCK), (0, 0), (0, 0))
    kp = jnp.pad(k, pad).reshape(B, nb + 2, A_BLOCK, Hkv, d)
    vp = jnp.pad(v, pad).reshape(B, nb + 2, A_BLOCK, Hkv, d)
    kb = jnp.concatenate([kp[:, :-2], kp[:, 1:-1], kp[:, 2:]], axis=2)
    vb = jnp.concatenate([vp[:, :-2], vp[:, 1:-1], vp[:, 2:]], axis=2)
    qb = q.reshape(B, nb, A_BLOCK, Hkv, G, d)
    s = jnp.einsum('bnqhgd,bnshd->bnhgqs', qb, kb).astype(jnp.float32) * (d ** -0.5)
    i = jnp.arange(A_BLOCK)[:, None]
    j = jnp.arange(3 * A_BLOCK)[None, :]
    rel = A_BLOCK + i - j
    spos = jnp.arange(nb)[:, None] * A_BLOCK - A_BLOCK + jnp.arange(3 * A_BLOCK)[None, :]
    valid = (jnp.abs(rel) <= A_WINDOW)[None] & ((spos >= 0) & (spos < S))[:, None, :]
    slopes = _alibi_slopes(Hq).reshape(Hkv, G)
    alibi = -slopes[:, :, None, None] * jnp.abs(rel).astype(jnp.float32)[None, None]
    s = jnp.where(valid[None, :, None, None], s + alibi[None, None], NEG)
    sk = jnp.broadcast_to(sink.astype(jnp.float32).reshape(Hkv, G, 1, 1), s.shape[:-1] + (1,))
    p = jax.nn.softmax(jnp.concatenate([s, sk], axis=-1), axis=-1)[..., :-1]
    o = jnp.einsum('bnhgqs,bnshd->bnqhgd', p.astype(v.dtype), vb)
    return o.reshape(B, S, Hq * d)


def _neighbourhood_attn(q, k, v, rpb):
    B, S, H, d = q.shape
    rows = S // GRID_W
    kh = min(NB_ROWS_MAX, rows)
    r = jnp.arange(rows)
    r0 = jnp.clip(r - kh // 2, 0, rows - kh)
    row_idx = r0[:, None] + jnp.arange(kh)[None, :]
    c = jnp.arange(GRID_W)
    c0 = jnp.clip(c - NB_COLS // 2, 0, GRID_W - NB_COLS)
    col_ok = (c[None, :] >= c0[:, None]) & (c[None, :] < c0[:, None] + NB_COLS)
    qg = q.reshape(B, rows, GRID_W, H, d)
    kg = jnp.take(k.reshape(B, rows, GRID_W, H, d), row_idx, axis=1)
    vg = jnp.take(v.reshape(B, rows, GRID_W, H, d), row_idx, axis=1)
    s = jnp.einsum('brqhd,brjkhd->brhqjk', qg, kg).astype(jnp.float32) * (d ** -0.5)
    dr = row_idx - r[:, None] + (NB_ROWS_MAX - 1)
    dc = jnp.clip(c[None, :] - c[:, None], -(NB_COLS - 1), NB_COLS - 1) + (NB_COLS - 1)
    bias = rpb.astype(jnp.float32)[:, dr[:, None, :, None], dc[None, :, None, :]]
    s = s + jnp.transpose(bias, (1, 0, 2, 3, 4))[None]
    s = jnp.where(col_ok[None, None, None, :, None, :], s, NEG)
    p = jax.nn.softmax(s.reshape(B, rows, H, GRID_W, kh * GRID_W), axis=-1)
    p = p.reshape(B, rows, H, GRID_W, kh, GRID_W).astype(v.dtype)
    o = jnp.einsum('brhqjk,brjkhd->brqhd', p, vg)
    return o.reshape(B, S, H * d)


def _layer(x, c, w_ada, b_ada, ln1, ln2, w_in, qn_a, kn_a, qn_b, kn_b, sink_a, rpb_b,
           w_br_a, w_br_b, w_out, w_mlp1, w_mlp2):
    B, S, _ = x.shape
    mod = jax.nn.silu(c) @ w_ada + b_ada
    sh1, sc1, g1, sh2, sc2, g2 = jnp.split(mod[:, None, :], 6, axis=-1)
    h = _rms(x, ln1) * (1 + sc1) + sh1
    z = h @ w_in
    cuts = [A_Q, A_Q + A_KV, A_Q + 2 * A_KV, A_Q + 2 * A_KV + B_W,
            A_Q + 2 * A_KV + 2 * B_W, A_Q + 2 * A_KV + 3 * B_W,
            A_Q + 2 * A_KV + 3 * B_W + D_MODEL]
    qa, ka, va, qb, kb, vb, ga, gb = jnp.split(z, cuts, axis=-1)
    qa = _rms(qa.reshape(B, S, A_HEADS, HEAD_DIM), qn_a)
    ka = _rms(ka.reshape(B, S, A_KV_HEADS, HEAD_DIM), kn_a)
    va = va.reshape(B, S, A_KV_HEADS, HEAD_DIM)
    qb = _rms(qb.reshape(B, S, B_HEADS, HEAD_DIM), qn_b)
    kb = _rms(kb.reshape(B, S, B_HEADS, HEAD_DIM), kn_b)
    vb = vb.reshape(B, S, B_HEADS, HEAD_DIM)
    oa = _window_gqa(qa, ka, va, sink_a) @ w_br_a
    ob = _neighbourhood_attn(qb, kb, vb, rpb_b) @ w_br_b
    m = jax.nn.sigmoid(ga) * oa + jax.nn.sigmoid(gb) * ob
    x = x + g1 * (m @ w_out)
    h2 = _rms(x, ln2) * (1 + sc2) + sh2
    f = jnp.square(jax.nn.relu(h2 @ w_mlp1)) @ w_mlp2
    return x + g2 * f


def _trunk(x, c, w_ada, b_ada, ln1, ln2, w_in, qn_a, kn_a, qn_b, kn_b, sink_a, rpb_b,
           w_br_a, w_br_b, w_out, w_mlp1, w_mlp2):
    for l in range(DEPTH):
        x = _layer(x, c, w_ada[l], b_ada[l], ln1[l], ln2[l], w_in[l], qn_a[l], kn_a[l],
                   qn_b[l], kn_b[l], sink_a[l], rpb_b[l], w_br_a[l], w_br_b[l], w_out[l],
                   w_mlp1[l], w_mlp2[l])
    return x


def setup_inputs(seed: int = 0) -> dict:
    key = jax.random.key(seed)
    ks = jax.random.split(key, 24)
    f32 = jnp.float32

    def nrm(k, shape, scale):
        return jax.random.normal(k, shape, f32) * scale

    return {
        "x_prompt": nrm(ks[0], (BATCH, SEQ, D_MODEL), 1.0),
        "x_sample": nrm(ks[1], (DEC_BATCH, DEC_SEQ, D_MODEL), 1.0),
        "c_prompt": nrm(ks[2], (BATCH, D_MODEL), 1.0),
        "c_sample": nrm(ks[3], (DEC_BATCH, D_MODEL), 1.0),
        "w_ada": nrm(ks[4], (DEPTH, D_MODEL, 6 * D_MODEL), 0.5 * D_MODEL ** -0.5),
        "b_ada": nrm(ks[5], (DEPTH, 6 * D_MODEL), 0.02),
        "ln1": 1.0 + nrm(ks[6], (DEPTH, D_MODEL), 0.02),
        "ln2": 1.0 + nrm(ks[7], (DEPTH, D_MODEL), 0.02),
        "w_in": nrm(ks[8], (DEPTH, D_MODEL, IN_COLS), D_MODEL ** -0.5),
        "qn_a": 1.0 + nrm(ks[9], (DEPTH, HEAD_DIM), 0.02),
        "kn_a": 1.0 + nrm(ks[10], (DEPTH, HEAD_DIM), 0.02),
        "qn_b": 1.0 + nrm(ks[11], (DEPTH, HEAD_DIM), 0.02),
        "kn_b": 1.0 + nrm(ks[12], (DEPTH, HEAD_DIM), 0.02),
        "sink_a": nrm(ks[13], (DEPTH, A_HEADS), 0.5),
        "rpb_b": nrm(ks[14], (DEPTH, B_HEADS, 2 * NB_ROWS_MAX - 1, 2 * NB_COLS - 1), 0.1),
        "w_br_a": nrm(ks[15], (DEPTH, A_Q, D_MODEL), A_Q ** -0.5),
        "w_br_b": nrm(ks[16], (DEPTH, B_W, D_MODEL), B_W ** -0.5),
        "w_out": nrm(ks[17], (DEPTH, D_MODEL, D_MODEL), D_MODEL ** -0.5),
        "w_mlp1": nrm(ks[18], (DEPTH, D_MODEL, D_FF), D_MODEL ** -0.5),
        "w_mlp2": nrm(ks[19], (DEPTH, D_FF, D_MODEL), D_FF ** -0.5),
    }


def reference(x_prompt, x_sample, c_prompt, c_sample, w_ada, b_ada, ln1, ln2, w_in,
              qn_a, kn_a, qn_b, kn_b, sink_a, rpb_b, w_br_a, w_br_b, w_out, w_mlp1, w_mlp2):
    y_prompt = _trunk(x_prompt, c_prompt, w_ada, b_ada, ln1, ln2, w_in, qn_a, kn_a, qn_b, kn_b,
                      sink_a, rpb_b, w_br_a, w_br_b, w_out, w_mlp1, w_mlp2)
    y_sample = _trunk(x_sample, c_sample, w_ada, b_ada, ln1, ln2, w_in, qn_a, kn_a, qn_b, kn_b,
                      sink_a, rpb_b, w_br_a, w_br_b, w_out, w_mlp1, w_mlp2)
    return (y_prompt, y_sample)
```

```python
import functools
import math

import jax
import jax.numpy as jnp
import numpy as np
from jax import lax
from jax.experimental import pallas as pl
from jax.experimental.pallas import tpu as pltpu

F32 = jnp.float32
BF16 = jnp.bfloat16

D_MODEL = 2048
HEAD_DIM = 128
A_HEADS = 8
A_KV_HEADS = 2
A_GROUP = A_HEADS // A_KV_HEADS
A_BLOCK = 128
B_HEADS = 8
GRID_W = 64
NB_ROWS = 8
NB_COLS = 16
D_FF = 4 * D_MODEL
EPS = 1e-6
NEG = -1e30
SCALE = HEAD_DIM ** -0.5

Z_GA = 0
Z_GB = Z_GA + D_MODEL
Z_QA = Z_GB + D_MODEL
Z_KA = Z_QA + A_HEADS * HEAD_DIM
Z_VA = Z_KA + A_KV_HEADS * HEAD_DIM
Z_QB = Z_VA + A_KV_HEADS * HEAD_DIM
Z_KB = Z_QB + B_HEADS * HEAD_DIM
Z_VB = Z_KB + B_HEADS * HEAD_DIM
Z_COLS = Z_VB + B_HEADS * HEAD_DIM

LANES = 128
V7X_VMEM_BUDGET_BYTES = 56 * 1024 * 1024

TM_PROJ = 1024
TN_PROJ = 512
TM_MIX = 512
TM_MLP = 512
TF_MLP = 1024
TQ_A = 512
TN_ADA = 1024


def _compiler_params(semantics, vmem_bytes):
    limit = int(min(max(vmem_bytes, 16 * 1024 * 1024), V7X_VMEM_BUDGET_BYTES))
    return pltpu.CompilerParams(dimension_semantics=semantics, vmem_limit_bytes=limit)


def _nbytes(shape, dtype):
    return int(np.prod(shape)) * jnp.dtype(dtype).itemsize


def _sigmoid(x):
    return 1.0 / (1.0 + jnp.exp(-x))


def _adaln_kernel(c_ref, w_ref, b_ref, o_ref):
    c = c_ref[...]
    a = (c * _sigmoid(c)).astype(BF16)
    w = w_ref[0].astype(BF16)
    o_ref[0] = jnp.dot(a, w, preferred_element_type=F32) + b_ref[0]


def _adaln(c, w_ada, b_ada):
    depth, d, n = w_ada.shape
    nb = c.shape[0]
    vmem = 2 * (_nbytes((d, TN_ADA), F32) + _nbytes((nb, d), F32) + _nbytes((nb, TN_ADA), F32))
    vmem += _nbytes((d, TN_ADA), BF16) + (4 << 20)
    return pl.pallas_call(
        _adaln_kernel,
        out_shape=jax.ShapeDtypeStruct((depth, nb, n), F32),
        grid=(depth, n // TN_ADA),
        in_specs=[
            pl.BlockSpec((nb, d), lambda l, j: (0, 0)),
            pl.BlockSpec((1, d, TN_ADA), lambda l, j: (l, 0, j)),
            pl.BlockSpec((1, 1, TN_ADA), lambda l, j: (l, 0, j)),
        ],
        out_specs=pl.BlockSpec((1, nb, TN_ADA), lambda l, j: (l, 0, j)),
        compiler_params=_compiler_params(("arbitrary", "arbitrary"), vmem),
        name="adaln",
    )(c, w_ada, b_ada.reshape(depth, 1, n))


def _modulated_rmsnorm(x, ln, shift, scale):
    ms = jnp.mean(x * x, axis=-1, keepdims=True)
    y = x * lax.rsqrt(ms + EPS)
    return y * (ln * (1.0 + scale)) + shift


def _in_proj_kernel(x_ref, sh_ref, sc_ref, ln_ref, w_ref, g_ref, f_ref, z_ref, h_scr, *,
                    norm_lo, norm_hi):
    j = pl.program_id(2)

    @pl.when(j == 0)
    def _():
        h = _modulated_rmsnorm(x_ref[0], ln_ref[...], sh_ref[0, 0], sc_ref[0, 0])
        h_scr[...] = h.astype(BF16)

    acc = jnp.dot(h_scr[...], w_ref[...], preferred_element_type=F32)
    is_norm_tile = jnp.logical_and(j >= norm_lo, j < norm_hi)

    @pl.when(is_norm_tile)
    def _():
        for c in range(TN_PROJ // LANES):
            sl = slice(c * LANES, (c + 1) * LANES)
            blk = acc[:, sl]
            r = lax.rsqrt(jnp.mean(blk * blk, axis=-1, keepdims=True) + EPS)
            fac = jnp.where(f_ref[:, sl] > 0.0, r * g_ref[:, sl], 1.0)
            z_ref[0, :, sl] = (blk * fac).astype(BF16)

    @pl.when(jnp.logical_not(is_norm_tile))
    def _():
        z_ref[0] = acc.astype(BF16)


def _in_proj(x, mod, ln, w_in, gains, flags):
    b, s, d = x.shape
    n = w_in.shape[1]
    assert s % TM_PROJ == 0 and n % TN_PROJ == 0
    assert Z_QA % TN_PROJ == 0
    norm_lo = Z_QA // TN_PROJ
    norm_hi = -(-Z_VB // TN_PROJ)
    vmem = 2 * (_nbytes((TM_PROJ, d), F32) + _nbytes((d, TN_PROJ), BF16) + _nbytes((TM_PROJ, TN_PROJ), BF16))
    vmem += _nbytes((TM_PROJ, d), BF16) + 3 * _nbytes((TM_PROJ, d), F32) // 2 + 2 * _nbytes((TM_PROJ, TN_PROJ), F32)
    kern = functools.partial(_in_proj_kernel, norm_lo=norm_lo, norm_hi=norm_hi)
    return pl.pallas_call(
        kern,
        out_shape=jax.ShapeDtypeStruct((b, s, n), BF16),
        grid=(b, s // TM_PROJ, n // TN_PROJ),
        in_specs=[
            pl.BlockSpec((1, TM_PROJ, d), lambda bi, i, j: (bi, i, 0)),
            pl.BlockSpec((1, 1, 1, d), lambda bi, i, j: (bi, 0, 0, 0)),
            pl.BlockSpec((1, 1, 1, d), lambda bi, i, j: (bi, 1, 0, 0)),
            pl.BlockSpec((1, d), lambda bi, i, j: (0, 0)),
            pl.BlockSpec((d, TN_PROJ), lambda bi, i, j: (0, j)),
            pl.BlockSpec((1, TN_PROJ), lambda bi, i, j: (0, j)),
            pl.BlockSpec((1, TN_PROJ), lambda bi, i, j: (0, j)),
        ],
        out_specs=pl.BlockSpec((1, TM_PROJ, TN_PROJ), lambda bi, i, j: (bi, i, j)),
        scratch_shapes=[pltpu.VMEM((TM_PROJ, d), BF16)],
        compiler_params=_compiler_params(("arbitrary", "arbitrary", "arbitrary"), vmem),
        name="in_proj",
    )(x, mod, mod, ln, w_in, gains, flags)


def _attn_a_kernel(sink_ref, q_ref, kp_ref, km_ref, kn_ref, vp_ref, vm_ref, vn_ref, bias_ref,
                   o_ref):
    hk = pl.program_id(1)
    n = pl.program_id(2)
    last = pl.num_programs(2) - 1
    kcat = jnp.concatenate([kp_ref[0], km_ref[0], kn_ref[0]], axis=0)
    vcat = jnp.concatenate([vp_ref[0], vm_ref[0], vn_ref[0]], axis=0)
    bias = bias_ref[0]
    col = lax.broadcasted_iota(jnp.int32, (1, 3 * A_BLOCK), 1)
    lo = jnp.where(n > 0, 0, A_BLOCK)
    hi = jnp.where(n < last, 3 * A_BLOCK, 2 * A_BLOCK)
    sink = jnp.concatenate(
        [jnp.full((A_BLOCK, 1), sink_ref[hk * A_GROUP + g], F32) for g in range(A_GROUP)], axis=0)
    for t in range(TQ_A // A_BLOCK):
        q = q_ref[0, t * A_BLOCK:(t + 1) * A_BLOCK, :]
        qs = jnp.concatenate([q[:, g * HEAD_DIM:(g + 1) * HEAD_DIM] for g in range(A_GROUP)], axis=0)
        kw = kcat[t * A_BLOCK:(t + 3) * A_BLOCK]
        vw = vcat[t * A_BLOCK:(t + 3) * A_BLOCK]
        s = lax.dot_general(qs, kw, (((1,), (1,)), ((), ())), preferred_element_type=F32)
        s = s * SCALE + bias
        if t == 0:
            s = jnp.where(col >= lo, s, NEG)
        if t == TQ_A // A_BLOCK - 1:
            s = jnp.where(col < hi, s, NEG)
        m = jnp.maximum(jnp.max(s, axis=-1, keepdims=True), sink)
        p = jnp.exp(s - m)
        den = jnp.sum(p, axis=-1, keepdims=True) + jnp.exp(sink - m)
        o = jnp.dot(p.astype(BF16), vw, preferred_element_type=F32) / den
        for g in range(A_GROUP):
            o_ref[0, t * A_BLOCK:(t + 1) * A_BLOCK, g * HEAD_DIM:(g + 1) * HEAD_DIM] = (
                o[g * A_BLOCK:(g + 1) * A_BLOCK].astype(BF16))


def _attn_a(z, sink, bias):
    b, s, _ = z.shape
    nblk = s // A_BLOCK
    per = TQ_A // A_BLOCK
    qw = A_GROUP * HEAD_DIM
    q0, k0, v0 = Z_QA // qw, Z_KA // HEAD_DIM, Z_VA // HEAD_DIM

    def prev_map(col0):
        return lambda bi, h, n: (bi, jnp.maximum(n * per - 1, 0), col0 + h)

    def main_map(col0):
        return lambda bi, h, n: (bi, n, col0 + h)

    def next_map(col0):
        return lambda bi, h, n: (bi, jnp.minimum((n + 1) * per, nblk - 1), col0 + h)

    edge = (1, A_BLOCK, HEAD_DIM)
    main = (1, TQ_A, HEAD_DIM)
    vmem = 2 * (2 * _nbytes((TQ_A, qw), BF16) + 2 * _nbytes((TQ_A + 2 * A_BLOCK, HEAD_DIM), BF16)
                + _nbytes((qw, 3 * A_BLOCK), F32)) + 8 * _nbytes((qw, 3 * A_BLOCK), F32) + (4 << 20)
    return pl.pallas_call(
        _attn_a_kernel,
        out_shape=jax.ShapeDtypeStruct((b, s, A_HEADS * HEAD_DIM), BF16),
        grid=(b, A_KV_HEADS, s // TQ_A),
        in_specs=[
            pl.BlockSpec(memory_space=pltpu.SMEM),
            pl.BlockSpec((1, TQ_A, qw), lambda bi, h, n: (bi, n, q0 + h)),
            pl.BlockSpec(edge, prev_map(k0)),
            pl.BlockSpec(main, main_map(k0)),
            pl.BlockSpec(edge, next_map(k0)),
            pl.BlockSpec(edge, prev_map(v0)),
            pl.BlockSpec(main, main_map(v0)),
            pl.BlockSpec(edge, next_map(v0)),
            pl.BlockSpec((1, qw, 3 * A_BLOCK), lambda bi, h, n: (h, 0, 0)),
        ],
        out_specs=pl.BlockSpec((1, TQ_A, qw), lambda bi, h, n: (bi, n, h)),
        compiler_params=_compiler_params(("arbitrary", "arbitrary", "arbitrary"), vmem),
        name="attn_a",
    )(sink, z, z, z, z, z, z, z, bias)


def _alibi_window_bias():
    i = np.arange(A_BLOCK)[:, None]
    j = np.arange(3 * A_BLOCK)[None, :]
    rel = np.abs(A_BLOCK + i - j).astype(np.float32)
    slopes = 2.0 ** (-8.0 * np.arange(1, A_HEADS + 1, dtype=np.float32) / A_HEADS)
    bias = np.where(rel[None] <= A_BLOCK, -slopes[:, None, None] * rel[None], np.float32(NEG))
    return jnp.asarray(bias.astype(np.float32).reshape(A_KV_HEADS, A_GROUP * A_BLOCK, 3 * A_BLOCK))


def _attn_b_kernel(q_ref, k_ref, v_ref, bias_ref, o_ref, *, rows, unroll):
    kh = NB_ROWS
    span = kh * GRID_W

    def one_row(r):
        r0 = jnp.clip(r - kh // 2, 0, rows - kh)
        qs = pl.multiple_of(r * GRID_W, GRID_W)
        ks = pl.multiple_of(r0 * GRID_W, GRID_W)
        q = q_ref[0, pl.ds(qs, GRID_W), :]
        kw = k_ref[0, pl.ds(ks, span), :]
        vw = v_ref[0, pl.ds(ks, span), :]
        s = lax.dot_general(q, kw, (((1,), (1,)), ((), ())), preferred_element_type=F32)
        s = s * SCALE + bias_ref[0, r - r0]
        m = jnp.max(s, axis=-1, keepdims=True)
        p = jnp.exp(s - m)
        den = jnp.sum(p, axis=-1, keepdims=True)
        o = jnp.dot(p.astype(BF16), vw, preferred_element_type=F32) / den
        o_ref[0, pl.ds(qs, GRID_W), :] = o.astype(BF16)

    def body(i, carry):
        for u in range(unroll):
            one_row(i * unroll + u)
        return carry

    lax.fori_loop(0, rows // unroll, body, 0)


def _attn_b(z, bias):
    b, s, _ = z.shape
    rows = s // GRID_W
    q0, k0, v0 = Z_QB // HEAD_DIM, Z_KB // HEAD_DIM, Z_VB // HEAD_DIM
    blk = (1, s, HEAD_DIM)
    vmem = 2 * (4 * _nbytes((s, HEAD_DIM), BF16) + _nbytes(bias.shape[1:], F32)) + (8 << 20)
    kern = functools.partial(_attn_b_kernel, rows=rows, unroll=4)
    return pl.pallas_call(
        kern,
        out_shape=jax.ShapeDtypeStruct((b, s, B_HEADS * HEAD_DIM), BF16),
        grid=(B_HEADS, b),
        in_specs=[
            pl.BlockSpec(blk, lambda h, bi: (bi, 0, q0 + h)),
            pl.BlockSpec(blk, lambda h, bi: (bi, 0, k0 + h)),
            pl.BlockSpec(blk, lambda h, bi: (bi, 0, v0 + h)),
            pl.BlockSpec((1,) + bias.shape[1:], lambda h, bi: (h, 0, 0, 0)),
        ],
        out_specs=pl.BlockSpec(blk, lambda h, bi: (bi, 0, h)),
        compiler_params=_compiler_params(("arbitrary", "arbitrary"), vmem),
        name="attn_b",
    )(z, z, z, bias)


def _neighbourhood_bias(rpb):
    kh = NB_ROWS
    c = np.arange(GRID_W)
    c0 = np.clip(c - NB_COLS // 2, 0, GRID_W - NB_COLS)
    col_ok = (c[None, :] >= c0[:, None]) & (c[None, :] < c0[:, None] + NB_COLS)
    dc = np.clip(c[None, :] - c[:, None], -(NB_COLS - 1), NB_COLS - 1) + (NB_COLS - 1)
    d = np.arange(kh)[:, None]
    j = np.arange(kh)[None, :]
    dr = j - d + (kh - 1)
    t = rpb.astype(F32)[:, dr[:, None, :, None], dc[None, :, None, :]]
    t = jnp.where(col_ok[None, None, :, None, :], t, NEG)
    return t.reshape(rpb.shape[0], kh, GRID_W, kh * GRID_W)


def _mix_kernel(oa_ref, ob_ref, ga_ref, gb_ref, wa_ref, wb_ref, m_ref):
    pa = jnp.dot(oa_ref[0], wa_ref[...], preferred_element_type=F32)
    pb = jnp.dot(ob_ref[0], wb_ref[...], preferred_element_type=F32)
    ga = _sigmoid(ga_ref[0].astype(F32))
    gb = _sigmoid(gb_ref[0].astype(F32))
    m_ref[0] = (ga * pa + gb * pb).astype(BF16)


def _mix(oa, ob, z, w_br_a, w_br_b):
    b, s, wa = oa.shape
    d = w_br_a.shape[1]
    assert Z_GA % d == 0 and Z_GB % d == 0
    tm = TM_MIX
    vmem = 2 * (2 * _nbytes((tm, wa), BF16) + 3 * _nbytes((tm, d), BF16) + 2 * _nbytes((wa, d), BF16))
    vmem += 5 * _nbytes((tm, d), F32)
    return pl.pallas_call(
        _mix_kernel,
        out_shape=jax.ShapeDtypeStruct((b, s, d), BF16),
        grid=(b, s // tm),
        in_specs=[
            pl.BlockSpec((1, tm, wa), lambda bi, i: (bi, i, 0)),
            pl.BlockSpec((1, tm, wa), lambda bi, i: (bi, i, 0)),
            pl.BlockSpec((1, tm, d), lambda bi, i: (bi, i, Z_GA // d)),
            pl.BlockSpec((1, tm, d), lambda bi, i: (bi, i, Z_GB // d)),
            pl.BlockSpec((wa, d), lambda bi, i: (0, 0)),
            pl.BlockSpec((wa, d), lambda bi, i: (0, 0)),
        ],
        out_specs=pl.BlockSpec((1, tm, d), lambda bi, i: (bi, i, 0)),
        compiler_params=_compiler_params(("arbitrary", "arbitrary"), vmem),
        name="mix",
    )(oa, ob, z, z, w_br_a, w_br_b)


def _out_proj_kernel(m_ref, x_ref, g_ref, w_ref, o_ref):
    y = jnp.dot(m_ref[0], w_ref[...], preferred_element_type=F32)
    o_ref[0] = x_ref[0] + g_ref[0, 0] * y


def _out_proj(m, x, mod, w_out):
    b, s, d = x.shape
    tm = TM_MIX
    vmem = 2 * (_nbytes((tm, d), BF16) + 2 * _nbytes((tm, d), F32) + _nbytes((d, d), BF16))
    vmem += 2 * _nbytes((tm, d), F32)
    return pl.pallas_call(
        _out_proj_kernel,
        out_shape=jax.ShapeDtypeStruct((b, s, d), F32),
        grid=(b, s // tm),
        in_specs=[
            pl.BlockSpec((1, tm, d), lambda bi, i: (bi, i, 0)),
            pl.BlockSpec((1, tm, d), lambda bi, i: (bi, i, 0)),
            pl.BlockSpec((1, 1, 1, d), lambda bi, i: (bi, 2, 0, 0)),
            pl.BlockSpec((d, d), lambda bi, i: (0, 0)),
        ],
        out_specs=pl.BlockSpec((1, tm, d), lambda bi, i: (bi, i, 0)),
        compiler_params=_compiler_params(("arbitrary", "arbitrary"), vmem),
        name="out_proj",
    )(m, x, mod, w_out)


def _mlp_kernel(x_ref, sh_ref, sc_ref, g_ref, ln_ref, w1_ref, w2_ref, o_ref, h_scr):
    f = pl.program_id(2)

    @pl.when(f == 0)
    def _():
        h = _modulated_rmsnorm(x_ref[0], ln_ref[...], sh_ref[0, 0], sc_ref[0, 0])
        h_scr[...] = h.astype(BF16)
        o_ref[...] = jnp.zeros_like(o_ref)

    u = jnp.dot(h_scr[...], w1_ref[...], preferred_element_type=F32)
    a = jnp.square(jnp.maximum(u, 0.0)).astype(BF16)
    o_ref[0] += jnp.dot(a, w2_ref[...], preferred_element_type=F32)

    @pl.when(f == pl.num_programs(2) - 1)
    def _():
        o_ref[0] = x_ref[0] + g_ref[0, 0] * o_ref[0]


def _mlp(x, mod, ln, w1, w2):
    b, s, d = x.shape
    ff = w1.shape[1]
    tm, tf = TM_MLP, TF_MLP
    vmem = 2 * (2 * _nbytes((tm, d), F32) + 2 * _nbytes((d, tf), BF16)) + _nbytes((tm, d), BF16)
    vmem += 2 * _nbytes((tm, tf), F32) + 2 * _nbytes((tm, d), F32)
    return pl.pallas_call(
        _mlp_kernel,
        out_shape=jax.ShapeDtypeStruct((b, s, d), F32),
        grid=(b, s // tm, ff // tf),
        in_specs=[
            pl.BlockSpec((1, tm, d), lambda bi, i, f: (bi, i, 0)),
            pl.BlockSpec((1, 1, 1, d), lambda bi, i, f: (bi, 3, 0, 0)),
            pl.BlockSpec((1, 1, 1, d), lambda bi, i, f: (bi, 4, 0, 0)),
            pl.BlockSpec((1, 1, 1, d), lambda bi, i, f: (bi, 5, 0, 0)),
            pl.BlockSpec((1, d), lambda bi, i, f: (0, 0)),
            pl.BlockSpec((d, tf), lambda bi, i, f: (0, f)),
            pl.BlockSpec((tf, d), lambda bi, i, f: (f, 0)),
        ],
        out_specs=pl.BlockSpec((1, tm, d), lambda bi, i, f: (bi, i, 0)),
        scratch_shapes=[pltpu.VMEM((tm, d), BF16)],
        compiler_params=_compiler_params(("arbitrary", "arbitrary", "arbitrary"), vmem),
        name="mlp",
    )(x, mod, mod, mod, ln, w1, w2)


def _split_w_in(w_in_l):
    aq, akv, bw = A_HEADS * HEAD_DIM, A_KV_HEADS * HEAD_DIM, B_HEADS * HEAD_DIM
    cuts = np.cumsum([aq, akv, akv, bw, bw, bw, D_MODEL])
    qa, ka, va, qb, kb, vb, ga, gb = jnp.split(w_in_l, cuts, axis=1)
    return jnp.concatenate([ga, gb, qa, ka, va, qb, kb, vb], axis=1).astype(BF16)


def _qk_norm_vectors(qn_a, kn_a, qn_b, kn_b):
    ones = jnp.ones((HEAD_DIM,), F32)
    chunks, flags = [], []
    for gain, count, on in (
        (ones, 2 * D_MODEL // HEAD_DIM, 0.0),
        (qn_a, A_HEADS, 1.0), (kn_a, A_KV_HEADS, 1.0), (ones, A_KV_HEADS, 0.0),
        (qn_b, B_HEADS, 1.0), (kn_b, B_HEADS, 1.0), (ones, B_HEADS, 0.0),
    ):
        chunks.append(jnp.tile(gain.astype(F32), count))
        flags.append(jnp.full((count * HEAD_DIM,), on, F32))
    return jnp.concatenate(chunks)[None, :], jnp.concatenate(flags)[None, :]


def kernel(x_prompt, x_sample, c_prompt, c_sample, w_ada, b_ada, ln1, ln2, w_in, qn_a, kn_a, qn_b,
           kn_b, sink_a, rpb_b, w_br_a, w_br_b, w_out, w_mlp1, w_mlp2):
    depth = w_ada.shape[0]
    nb_p = x_prompt.shape[0]
    c = jnp.concatenate([c_prompt, c_sample], axis=0)
    mod = _adaln(c, w_ada, b_ada)
    mod = mod.reshape(depth, c.shape[0], 6, 1, D_MODEL)
    alibi = _alibi_window_bias()
    xs = [x_prompt, x_sample]
    for l in range(depth):
        w_in_l = _split_w_in(w_in[l])
        gains, flags = _qk_norm_vectors(qn_a[l], kn_a[l], qn_b[l], kn_b[l])
        nbias = _neighbourhood_bias(rpb_b[l])
        wa, wb = w_br_a[l].astype(BF16), w_br_b[l].astype(BF16)
        wo, w1, w2 = w_out[l].astype(BF16), w_mlp1[l].astype(BF16), w_mlp2[l].astype(BF16)
        ln1_l, ln2_l = ln1[l][None, :], ln2[l][None, :]
        sink_l = sink_a[l].astype(F32)
        for g in range(2):
            x = xs[g]
            mod_g = mod[l, :nb_p] if g == 0 else mod[l, nb_p:]
            z = _in_proj(x, mod_g, ln1_l, w_in_l, gains, flags)
            oa = _attn_a(z, sink_l, alibi)
            ob = _attn_b(z, nbias)
            m = _mix(oa, ob, z, wa, wb)
            x = _out_proj(m, x, mod_g, wo)
            xs[g] = _mlp(x, mod_g, ln2_l, w1, w2)
    return (xs[0], xs[1])
```

```python
import functools
import math

import jax
import jax.numpy as jnp
import numpy as np
from jax import lax
from jax.experimental import pallas as pl
from jax.experimental.pallas import tpu as pltpu

F32 = jnp.float32
BF16 = jnp.bfloat16

D_MODEL = 2048
HEAD_DIM = 128
A_HEADS = 8
A_KV_HEADS = 2
A_GROUP = A_HEADS // A_KV_HEADS
A_BLOCK = 128
B_HEADS = 8
GRID_W = 64
NB_ROWS = 8
NB_COLS = 16
D_FF = 4 * D_MODEL
EPS = 1e-6
NEG = -1e30
SCALE = HEAD_DIM ** -0.5

Z_GA = 0
Z_GB = Z_GA + D_MODEL
Z_QA = Z_GB + D_MODEL
Z_KA = Z_QA + A_HEADS * HEAD_DIM
Z_VA = Z_KA + A_KV_HEADS * HEAD_DIM
Z_QB = Z_VA + A_KV_HEADS * HEAD_DIM
Z_KB = Z_QB + B_HEADS * HEAD_DIM
Z_VB = Z_KB + B_HEADS * HEAD_DIM
Z_COLS = Z_VB + B_HEADS * HEAD_DIM

LANES = 128
V7X_VMEM_BUDGET_BYTES = 56 * 1024 * 1024

TM_PROJ = 1024
TN_PROJ = 512
TM_MIX = 512
TM_MLP = 512
TF_MLP = 1024
TQ_A = 512
TN_ADA = 1024


def _compiler_params(semantics, vmem_bytes):
    limit = int(min(max(vmem_bytes, 16 * 1024 * 1024), V7X_VMEM_BUDGET_BYTES))
    return pltpu.CompilerParams(dimension_semantics=semantics, vmem_limit_bytes=limit)


def _nbytes(shape, dtype):
    return int(np.prod(shape)) * jnp.dtype(dtype).itemsize


def _sigmoid(x):
    return 1.0 / (1.0 + jnp.exp(-x))


def _adaln_kernel(c_ref, w_ref, b_ref, o_ref):
    c = c_ref[...]
    a = (c * _sigmoid(c)).astype(BF16)
    w = w_ref[0].astype(BF16)
    o_ref[0] = jnp.dot(a, w, preferred_element_type=F32) + b_ref[0]


def _adaln(c, w_ada, b_ada):
    depth, d, n = w_ada.shape
    nb = c.shape[0]
    vmem = 2 * (_nbytes((d, TN_ADA), F32) + _nbytes((nb, d), F32) + _nbytes((nb, TN_ADA), F32))
    vmem += _nbytes((d, TN_ADA), BF16) + (4 << 20)
    return pl.pallas_call(
        _adaln_kernel,
        out_shape=jax.ShapeDtypeStruct((depth, nb, n), F32),
        grid=(depth, n // TN_ADA),
        in_specs=[
            pl.BlockSpec((nb, d), lambda l, j: (0, 0)),
            pl.BlockSpec((1, d, TN_ADA), lambda l, j: (l, 0, j)),
            pl.BlockSpec((1, 1, TN_ADA), lambda l, j: (l, 0, j)),
        ],
        out_specs=pl.BlockSpec((1, nb, TN_ADA), lambda l, j: (l, 0, j)),
        compiler_params=_compiler_params(("arbitrary", "arbitrary"), vmem),
        name="adaln",
    )(c, w_ada, b_ada.reshape(depth, 1, n))


def _modulated_rmsnorm(x, ln, shift, scale):
    ms = jnp.mean(x * x, axis=-1, keepdims=True)
    y = x * lax.rsqrt(ms + EPS)
    return y * (ln * (1.0 + scale)) + shift


def _in_proj_kernel(x_ref, sh_ref, sc_ref, ln_ref, w_ref, g_ref, f_ref, z_ref, h_scr, *,
                    norm_lo, norm_hi):
    j = pl.program_id(2)

    @pl.when(j == 0)
    def _():
        h = _modulated_rmsnorm(x_ref[0], ln_ref[...], sh_ref[0, 0], sc_ref[0, 0])
        h_scr[...] = h.astype(BF16)

    acc = jnp.dot(h_scr[...], w_ref[...], preferred_element_type=F32)
    is_norm_tile = jnp.logical_and(j >= norm_lo, j < norm_hi)

    @pl.when(is_norm_tile)
    def _():
        for c in range(TN_PROJ // LANES):
            sl = slice(c * LANES, (c + 1) * LANES)
            blk = acc[:, sl]
            r = lax.rsqrt(jnp.mean(blk * blk, axis=-1, keepdims=True) + EPS)
            fac = jnp.where(f_ref[:, sl] > 0.0, r * g_ref[:, sl], 1.0)
            z_ref[0, :, sl] = (blk * fac).astype(BF16)

    @pl.when(jnp.logical_not(is_norm_tile))
    def _():
        z_ref[0] = acc.astype(BF16)


def _in_proj(x, mod, ln, w_in, gains, flags):
    b, s, d = x.shape
    n = w_in.shape[1]
    assert s % TM_PROJ == 0 and n % TN_PROJ == 0
    assert Z_QA % TN_PROJ == 0
    norm_lo = Z_QA // TN_PROJ
    norm_hi = -(-Z_VB // TN_PROJ)
    vmem = 2 * (_nbytes((TM_PROJ, d), F32) + _nbytes((d, TN_PROJ), BF16) + _nbytes((TM_PROJ, TN_PROJ), BF16))
    vmem += _nbytes((TM_PROJ, d), BF16) + 3 * _nbytes((TM_PROJ, d), F32) // 2 + 2 * _nbytes((TM_PROJ, TN_PROJ), F32)
    kern = functools.partial(_in_proj_kernel, norm_lo=norm_lo, norm_hi=norm_hi)
    return pl.pallas_call(
        kern,
        out_shape=jax.ShapeDtypeStruct((b, s, n), BF16),
        grid=(b, s // TM_PROJ, n // TN_PROJ),
        in_specs=[
            pl.BlockSpec((1, TM_PROJ, d), lambda bi, i, j: (bi, i, 0)),
            pl.BlockSpec((1, 1, 1, d), lambda bi, i, j: (bi, 0, 0, 0)),
            pl.BlockSpec((1, 1, 1, d), lambda bi, i, j: (bi, 1, 0, 0)),
            pl.BlockSpec((1, d), lambda bi, i, j: (0, 0)),
            pl.BlockSpec((d, TN_PROJ), lambda bi, i, j: (0, j)),
            pl.BlockSpec((1, TN_PROJ), lambda bi, i, j: (0, j)),
            pl.BlockSpec((1, TN_PROJ), lambda bi, i, j: (0, j)),
        ],
        out_specs=pl.BlockSpec((1, TM_PROJ, TN_PROJ), lambda bi, i, j: (bi, i, j)),
        scratch_shapes=[pltpu.VMEM((TM_PROJ, d), BF16)],
        compiler_params=_compiler_params(("arbitrary", "arbitrary", "arbitrary"), vmem),
        name="in_proj",
    )(x, mod, mod, ln, w_in, gains, flags)


def _attn_a_kernel(sink_ref, q_ref, kp_ref, km_ref, kn_ref, vp_ref, vm_ref, vn_ref, bias_ref,
                   o_ref):
    hk = pl.program_id(1)
    n = pl.program_id(2)
    last = pl.num_programs(2) - 1
    kcat = jnp.concatenate([kp_ref[0], km_ref[0], kn_ref[0]], axis=0)
    vcat = jnp.concatenate([vp_ref[0], vm_ref[0], vn_ref[0]], axis=0)
    bias = bias_ref[0]
    sink = jnp.concatenate(
        [jnp.full((1, A_BLOCK), sink_ref[hk * A_GROUP + g], F32) for g in range(A_GROUP)], axis=1)
    nt = TQ_A // A_BLOCK
    for t in range(nt):
        q = q_ref[0, t * A_BLOCK:(t + 1) * A_BLOCK, :]
        qs = jnp.concatenate([q[:, g * HEAD_DIM:(g + 1) * HEAD_DIM] for g in range(A_GROUP)], axis=0)
        kw = kcat[t * A_BLOCK:(t + 3) * A_BLOCK]
        vw = vcat[t * A_BLOCK:(t + 3) * A_BLOCK]
        s = lax.dot_general(kw, qs, (((1,), (1,)), ((), ())), preferred_element_type=F32)
        s = s * SCALE + bias
        if t == 0:
            s = jnp.concatenate([jnp.where(n > 0, s[:A_BLOCK], NEG), s[A_BLOCK:]], axis=0)
        if t == nt - 1:
            s = jnp.concatenate([s[:2 * A_BLOCK], jnp.where(n < last, s[2 * A_BLOCK:], NEG)], axis=0)
        m = jnp.maximum(jnp.max(s, axis=0, keepdims=True), sink)
        p = jnp.exp(s - m)
        den = jnp.sum(p, axis=0, keepdims=True) + jnp.exp(sink - m)
        ot = lax.dot_general(vw, p.astype(BF16), (((0,), (0,)), ((), ())),
                             preferred_element_type=F32)
        o = (ot / den).T
        for g in range(A_GROUP):
            o_ref[0, t * A_BLOCK:(t + 1) * A_BLOCK, g * HEAD_DIM:(g + 1) * HEAD_DIM] = (
                o[g * A_BLOCK:(g + 1) * A_BLOCK].astype(BF16))


def _attn_a(z, sink, bias):
    b, s, _ = z.shape
    nblk = s // A_BLOCK
    per = TQ_A // A_BLOCK
    qw = A_GROUP * HEAD_DIM
    q0, k0, v0 = Z_QA // qw, Z_KA // HEAD_DIM, Z_VA // HEAD_DIM

    def prev_map(col0):
        return lambda bi, h, n: (bi, jnp.maximum(n * per - 1, 0), col0 + h)

    def main_map(col0):
        return lambda bi, h, n: (bi, n, col0 + h)

    def next_map(col0):
        return lambda bi, h, n: (bi, jnp.minimum((n + 1) * per, nblk - 1), col0 + h)

    edge = (1, A_BLOCK, HEAD_DIM)
    main = (1, TQ_A, HEAD_DIM)
    vmem = 2 * (2 * _nbytes((TQ_A, qw), BF16) + 2 * _nbytes((TQ_A + 2 * A_BLOCK, HEAD_DIM), BF16)
                + _nbytes((qw, 3 * A_BLOCK), F32)) + 8 * _nbytes((qw, 3 * A_BLOCK), F32) + (4 << 20)
    return pl.pallas_call(
        _attn_a_kernel,
        out_shape=jax.ShapeDtypeStruct((b, s, A_HEADS * HEAD_DIM), BF16),
        grid=(b, A_KV_HEADS, s // TQ_A),
        in_specs=[
            pl.BlockSpec(memory_space=pltpu.SMEM),
            pl.BlockSpec((1, TQ_A, qw), lambda bi, h, n: (bi, n, q0 + h)),
            pl.BlockSpec(edge, prev_map(k0)),
            pl.BlockSpec(main, main_map(k0)),
            pl.BlockSpec(edge, next_map(k0)),
            pl.BlockSpec(edge, prev_map(v0)),
            pl.BlockSpec(main, main_map(v0)),
            pl.BlockSpec(edge, next_map(v0)),
            pl.BlockSpec((1, 3 * A_BLOCK, qw), lambda bi, h, n: (h, 0, 0)),
        ],
        out_specs=pl.BlockSpec((1, TQ_A, qw), lambda bi, h, n: (bi, n, h)),
        compiler_params=_compiler_params(("arbitrary", "arbitrary", "arbitrary"), vmem),
        name="attn_a",
    )(sink, z, z, z, z, z, z, z, bias)


def _alibi_window_bias():
    i = np.arange(A_BLOCK)[None, :]
    j = np.arange(3 * A_BLOCK)[:, None]
    rel = np.abs(A_BLOCK + i - j).astype(np.float32)
    slopes = 2.0 ** (-8.0 * np.arange(1, A_HEADS + 1, dtype=np.float32) / A_HEADS)
    bias = np.where(rel[None] <= A_BLOCK, -slopes[:, None, None] * rel[None], np.float32(NEG))
    bias = bias.astype(np.float32).reshape(A_KV_HEADS, A_GROUP, 3 * A_BLOCK, A_BLOCK)
    bias = bias.transpose(0, 2, 1, 3).reshape(A_KV_HEADS, 3 * A_BLOCK, A_GROUP * A_BLOCK)
    return jnp.asarray(bias)


def _attn_b_kernel(q_ref, k_ref, v_ref, bias_ref, o_ref, *, rows, unroll):
    kh = NB_ROWS
    span = kh * GRID_W

    def one_row(r):
        r0 = jnp.clip(r - kh // 2, 0, rows - kh)
        qs = pl.multiple_of(r * GRID_W, GRID_W)
        ks = pl.multiple_of(r0 * GRID_W, GRID_W)
        q = q_ref[0, pl.ds(qs, GRID_W), :]
        kw = k_ref[0, pl.ds(ks, span), :]
        vw = v_ref[0, pl.ds(ks, span), :]
        s = lax.dot_general(q, kw, (((1,), (1,)), ((), ())), preferred_element_type=F32)
        s = s * SCALE + bias_ref[0, r - r0]
        m = jnp.max(s, axis=-1, keepdims=True)
        p = jnp.exp(s - m)
        den = jnp.sum(p, axis=-1, keepdims=True)
        o = jnp.dot(p.astype(BF16), vw, preferred_element_type=F32) / den
        o_ref[0, pl.ds(qs, GRID_W), :] = o.astype(BF16)

    def body(i, carry):
        for u in range(unroll):
            one_row(i * unroll + u)
        return carry

    lax.fori_loop(0, rows // unroll, body, 0)


def _attn_b(z, bias):
    b, s, _ = z.shape
    rows = s // GRID_W
    q0, k0, v0 = Z_QB // HEAD_DIM, Z_KB // HEAD_DIM, Z_VB // HEAD_DIM
    blk = (1, s, HEAD_DIM)
    vmem = 2 * (4 * _nbytes((s, HEAD_DIM), BF16) + _nbytes(bias.shape[1:], F32)) + (8 << 20)
    kern = functools.partial(_attn_b_kernel, rows=rows, unroll=4)
    return pl.pallas_call(
        kern,
        out_shape=jax.ShapeDtypeStruct((b, s, B_HEADS * HEAD_DIM), BF16),
        grid=(B_HEADS, b),
        in_specs=[
            pl.BlockSpec(blk, lambda h, bi: (bi, 0, q0 + h)),
            pl.BlockSpec(blk, lambda h, bi: (bi, 0, k0 + h)),
            pl.BlockSpec(blk, lambda h, bi: (bi, 0, v0 + h)),
            pl.BlockSpec((1,) + bias.shape[1:], lambda h, bi: (h, 0, 0, 0)),
        ],
        out_specs=pl.BlockSpec(blk, lambda h, bi: (bi, 0, h)),
        compiler_params=_compiler_params(("arbitrary", "arbitrary"), vmem),
        name="attn_b",
    )(z, z, z, bias)


def _neighbourhood_bias(rpb):
    kh = NB_ROWS
    c = np.arange(GRID_W)
    c0 = np.clip(c - NB_COLS // 2, 0, GRID_W - NB_COLS)
    col_ok = (c[None, :] >= c0[:, None]) & (c[None, :] < c0[:, None] + NB_COLS)
    dc = np.clip(c[None, :] - c[:, None], -(NB_COLS - 1), NB_COLS - 1) + (NB_COLS - 1)
    onehot = (dc[None] == np.arange(2 * NB_COLS - 1)[:, None, None]).astype(np.float32)
    cols = jnp.sum(rpb.astype(F32)[:, :, :, None, None] * onehot[None, None], axis=2)
    t = jnp.stack([cols[:, kh - 1 - d:2 * kh - 1 - d] for d in range(kh)], axis=1)
    t = jnp.transpose(t, (0, 1, 3, 2, 4))
    t = jnp.where(col_ok[None, None, :, None, :], t, NEG)
    return t.reshape(rpb.shape[0], kh, GRID_W, kh * GRID_W)


def _mix_kernel(oa_ref, ob_ref, ga_ref, gb_ref, wa_ref, wb_ref, m_ref):
    pa = jnp.dot(oa_ref[0], wa_ref[...], preferred_element_type=F32)
    pb = jnp.dot(ob_ref[0], wb_ref[...], preferred_element_type=F32)
    ga = _sigmoid(ga_ref[0].astype(F32))
    gb = _sigmoid(gb_ref[0].astype(F32))
    m_ref[0] = (ga * pa + gb * pb).astype(BF16)


def _mix(oa, ob, z, w_br_a, w_br_b):
    b, s, wa = oa.shape
    d = w_br_a.shape[1]
    assert Z_GA % d == 0 and Z_GB % d == 0
    tm = TM_MIX
    vmem = 2 * (2 * _nbytes((tm, wa), BF16) + 3 * _nbytes((tm, d), BF16) + 2 * _nbytes((wa, d), BF16))
    vmem += 5 * _nbytes((tm, d), F32)
    return pl.pallas_call(
        _mix_kernel,
        out_shape=jax.ShapeDtypeStruct((b, s, d), BF16),
        grid=(b, s // tm),
        in_specs=[
            pl.BlockSpec((1, tm, wa), lambda bi, i: (bi, i, 0)),
            pl.BlockSpec((1, tm, wa), lambda bi, i: (bi, i, 0)),
            pl.BlockSpec((1, tm, d), lambda bi, i: (bi, i, Z_GA // d)),
            pl.BlockSpec((1, tm, d), lambda bi, i: (bi, i, Z_GB // d)),
            pl.BlockSpec((wa, d), lambda bi, i: (0, 0)),
            pl.BlockSpec((wa, d), lambda bi, i: (0, 0)),
        ],
        out_specs=pl.BlockSpec((1, tm, d), lambda bi, i: (bi, i, 0)),
        compiler_params=_compiler_params(("arbitrary", "arbitrary"), vmem),
        name="mix",
    )(oa, ob, z, z, w_br_a, w_br_b)


def _out_proj_kernel(m_ref, x_ref, g_ref, w_ref, o_ref):
    y = jnp.dot(m_ref[0], w_ref[...], preferred_element_type=F32)
    o_ref[0] = x_ref[0] + g_ref[0, 0] * y


def _out_proj(m, x, mod, w_out):
    b, s, d = x.shape
    tm = TM_MIX
    vmem = 2 * (_nbytes((tm, d), BF16) + 2 * _nbytes((tm, d), F32) + _nbytes((d, d), BF16))
    vmem += 2 * _nbytes((tm, d), F32)
    return pl.pallas_call(
        _out_proj_kernel,
        out_shape=jax.ShapeDtypeStruct((b, s, d), F32),
        grid=(b, s // tm),
        in_specs=[
            pl.BlockSpec((1, tm, d), lambda bi, i: (bi, i, 0)),
            pl.BlockSpec((1, tm, d), lambda bi, i: (bi, i, 0)),
            pl.BlockSpec((1, 1, 1, d), lambda bi, i: (bi, 2, 0, 0)),
            pl.BlockSpec((d, d), lambda bi, i: (0, 0)),
        ],
        out_specs=pl.BlockSpec((1, tm, d), lambda bi, i: (bi, i, 0)),
        compiler_params=_compiler_params(("arbitrary", "arbitrary"), vmem),
        name="out_proj",
    )(m, x, mod, w_out)


def _mlp_kernel(x_ref, sh_ref, sc_ref, g_ref, ln_ref, w1_ref, w2_ref, o_ref, h_scr):
    f = pl.program_id(2)

    @pl.when(f == 0)
    def _():
        h = _modulated_rmsnorm(x_ref[0], ln_ref[...], sh_ref[0, 0], sc_ref[0, 0])
        h_scr[...] = h.astype(BF16)
        o_ref[...] = jnp.zeros_like(o_ref)

    u = jnp.dot(h_scr[...], w1_ref[...], preferred_element_type=F32)
    a = jnp.square(jnp.maximum(u, 0.0)).astype(BF16)
    o_ref[0] += jnp.dot(a, w2_ref[...], preferred_element_type=F32)

    @pl.when(f == pl.num_programs(2) - 1)
    def _():
        o_ref[0] = x_ref[0] + g_ref[0, 0] * o_ref[0]


def _mlp(x, mod, ln, w1, w2):
    b, s, d = x.shape
    ff = w1.shape[1]
    tm, tf = TM_MLP, TF_MLP
    vmem = 2 * (2 * _nbytes((tm, d), F32) + 2 * _nbytes((d, tf), BF16)) + _nbytes((tm, d), BF16)
    vmem += 2 * _nbytes((tm, tf), F32) + 2 * _nbytes((tm, d), F32)
    return pl.pallas_call(
        _mlp_kernel,
        out_shape=jax.ShapeDtypeStruct((b, s, d), F32),
        grid=(b, s // tm, ff // tf),
        in_specs=[
            pl.BlockSpec((1, tm, d), lambda bi, i, f: (bi, i, 0)),
            pl.BlockSpec((1, 1, 1, d), lambda bi, i, f: (bi, 3, 0, 0)),
            pl.BlockSpec((1, 1, 1, d), lambda bi, i, f: (bi, 4, 0, 0)),
            pl.BlockSpec((1, 1, 1, d), lambda bi, i, f: (bi, 5, 0, 0)),
            pl.BlockSpec((1, d), lambda bi, i, f: (0, 0)),
            pl.BlockSpec((d, tf), lambda bi, i, f: (0, f)),
            pl.BlockSpec((tf, d), lambda bi, i, f: (f, 0)),
        ],
        out_specs=pl.BlockSpec((1, tm, d), lambda bi, i, f: (bi, i, 0)),
        scratch_shapes=[pltpu.VMEM((tm, d), BF16)],
        compiler_params=_compiler_params(("arbitrary", "arbitrary", "arbitrary"), vmem),
        name="mlp",
    )(x, mod, mod, mod, ln, w1, w2)


def _split_w_in(w_in_l):
    aq, akv, bw = A_HEADS * HEAD_DIM, A_KV_HEADS * HEAD_DIM, B_HEADS * HEAD_DIM
    cuts = np.cumsum([aq, akv, akv, bw, bw, bw, D_MODEL])
    qa, ka, va, qb, kb, vb, ga, gb = jnp.split(w_in_l, cuts, axis=1)
    return jnp.concatenate([ga, gb, qa, ka, va, qb, kb, vb], axis=1).astype(BF16)


def _qk_norm_vectors(qn_a, kn_a, qn_b, kn_b):
    ones = jnp.ones((HEAD_DIM,), F32)
    chunks, flags = [], []
    for gain, count, on in (
        (ones, 2 * D_MODEL // HEAD_DIM, 0.0),
        (qn_a, A_HEADS, 1.0), (kn_a, A_KV_HEADS, 1.0), (ones, A_KV_HEADS, 0.0),
        (qn_b, B_HEADS, 1.0), (kn_b, B_HEADS, 1.0), (ones, B_HEADS, 0.0),
    ):
        chunks.append(jnp.tile(gain.astype(F32), count))
        flags.append(jnp.full((count * HEAD_DIM,), on, F32))
    return jnp.concatenate(chunks)[None, :], jnp.concatenate(flags)[None, :]


def kernel(x_prompt, x_sample, c_prompt, c_sample, w_ada, b_ada, ln1, ln2, w_in, qn_a, kn_a, qn_b,
           kn_b, sink_a, rpb_b, w_br_a, w_br_b, w_out, w_mlp1, w_mlp2):
    depth = w_ada.shape[0]
    nb_p = x_prompt.shape[0]
    c = jnp.concatenate([c_prompt, c_sample], axis=0)
    mod = _adaln(c, w_ada, b_ada)
    mod = mod.reshape(depth, c.shape[0], 6, 1, D_MODEL)
    alibi = _alibi_window_bias()
    xs = [x_prompt, x_sample]
    for l in range(depth):
        w_in_l = _split_w_in(w_in[l])
        gains, flags = _qk_norm_vectors(qn_a[l], kn_a[l], qn_b[l], kn_b[l])
        nbias = _neighbourhood_bias(rpb_b[l])
        wa, wb = w_br_a[l].astype(BF16), w_br_b[l].astype(BF16)
        wo, w1, w2 = w_out[l].astype(BF16), w_mlp1[l].astype(BF16), w_mlp2[l].astype(BF16)
        ln1_l, ln2_l = ln1[l][None, :], ln2[l][None, :]
        sink_l = sink_a[l].astype(F32)
        for g in range(2):
            x = xs[g]
            mod_g = mod[l, :nb_p] if g == 0 else mod[l, nb_p:]
            z = _in_proj(x, mod_g, ln1_l, w_in_l, gains, flags)
            oa = _attn_a(z, sink_l, alibi)
            ob = _attn_b(z, nbias)
            m = _mix(oa, ob, z, wa, wb)
            x = _out_proj(m, x, mod_g, wo)
            xs[g] = _mlp(x, mod_g, ln2_l, w1, w2)
    return (xs[0], xs[1])
```

```python
import functools
import math

import jax
import jax.numpy as jnp
import numpy as np
from jax import lax
from jax.experimental import pallas as pl
from jax.experimental.pallas import tpu as pltpu

F32 = jnp.float32
BF16 = jnp.bfloat16

D_MODEL = 2048
HEAD_DIM = 128
A_HEADS = 8
A_KV_HEADS = 2
A_GROUP = A_HEADS // A_KV_HEADS
A_BLOCK = 128
B_HEADS = 8
GRID_W = 64
NB_ROWS = 8
NB_COLS = 16
D_FF = 4 * D_MODEL
EPS = 1e-6
NEG = -1e30
SCALE = HEAD_DIM ** -0.5

Z_GA = 0
Z_GB = Z_GA + D_MODEL
Z_QA = Z_GB + D_MODEL
Z_KA = Z_QA + A_HEADS * HEAD_DIM
Z_VA = Z_KA + A_KV_HEADS * HEAD_DIM
Z_QB = Z_VA + A_KV_HEADS * HEAD_DIM
Z_KB = Z_QB + B_HEADS * HEAD_DIM
Z_VB = Z_KB + B_HEADS * HEAD_DIM
Z_COLS = Z_VB + B_HEADS * HEAD_DIM

LANES = 128
V7X_VMEM_BUDGET_BYTES = 56 * 1024 * 1024

TM_PROJ = 1024
TN_PROJ = 512
TM_MIX = 512
TM_MLP = 512
TF_MLP = 1024
TQ_A = 512
TN_ADA = 1024


def _compiler_params(semantics, vmem_bytes):
    limit = int(min(max(vmem_bytes, 16 * 1024 * 1024), V7X_VMEM_BUDGET_BYTES))
    return pltpu.CompilerParams(dimension_semantics=semantics, vmem_limit_bytes=limit)


def _nbytes(shape, dtype):
    return int(np.prod(shape)) * jnp.dtype(dtype).itemsize


def _sigmoid(x):
    return 1.0 / (1.0 + jnp.exp(-x))


def _adaln_kernel(c_ref, w_ref, b_ref, o_ref):
    c = c_ref[...]
    a = (c * _sigmoid(c)).astype(BF16)
    w = w_ref[0].astype(BF16)
    o_ref[0] = jnp.dot(a, w, preferred_element_type=F32) + b_ref[0]


def _adaln(c, w_ada, b_ada):
    depth, d, n = w_ada.shape
    nb = c.shape[0]
    vmem = 2 * (_nbytes((d, TN_ADA), F32) + _nbytes((nb, d), F32) + _nbytes((nb, TN_ADA), F32))
    vmem += _nbytes((d, TN_ADA), BF16) + (4 << 20)
    return pl.pallas_call(
        _adaln_kernel,
        out_shape=jax.ShapeDtypeStruct((depth, nb, n), F32),
        grid=(depth, n // TN_ADA),
        in_specs=[
            pl.BlockSpec((nb, d), lambda l, j: (0, 0)),
            pl.BlockSpec((1, d, TN_ADA), lambda l, j: (l, 0, j)),
            pl.BlockSpec((1, 1, TN_ADA), lambda l, j: (l, 0, j)),
        ],
        out_specs=pl.BlockSpec((1, nb, TN_ADA), lambda l, j: (l, 0, j)),
        compiler_params=_compiler_params(("arbitrary", "arbitrary"), vmem),
        name="adaln",
    )(c, w_ada, b_ada.reshape(depth, 1, n))


def _modulated_rmsnorm(x, ln, shift, scale):
    ms = jnp.mean(x * x, axis=-1, keepdims=True)
    y = x * lax.rsqrt(ms + EPS)
    return y * (ln * (1.0 + scale)) + shift


def _in_proj_kernel(x_ref, sh_ref, sc_ref, ln_ref, w_ref, g_ref, f_ref, z_ref, h_scr, *,
                    norm_lo, norm_hi):
    j = pl.program_id(2)

    @pl.when(j == 0)
    def _():
        h = _modulated_rmsnorm(x_ref[0], ln_ref[...], sh_ref[0, 0], sc_ref[0, 0])
        h_scr[...] = h.astype(BF16)

    acc = jnp.dot(h_scr[...], w_ref[...], preferred_element_type=F32)
    is_norm_tile = jnp.logical_and(j >= norm_lo, j < norm_hi)

    @pl.when(is_norm_tile)
    def _():
        for c in range(TN_PROJ // LANES):
            sl = slice(c * LANES, (c + 1) * LANES)
            blk = acc[:, sl]
            r = lax.rsqrt(jnp.mean(blk * blk, axis=-1, keepdims=True) + EPS)
            fac = jnp.where(f_ref[:, sl] > 0.0, r * g_ref[:, sl], 1.0)
            z_ref[0, :, sl] = (blk * fac).astype(BF16)

    @pl.when(jnp.logical_not(is_norm_tile))
    def _():
        z_ref[0] = acc.astype(BF16)


def _in_proj(x, mod, ln, w_in, gains, flags):
    b, s, d = x.shape
    n = w_in.shape[1]
    assert s % TM_PROJ == 0 and n % TN_PROJ == 0
    assert Z_QA % TN_PROJ == 0
    norm_lo = Z_QA // TN_PROJ
    norm_hi = -(-Z_VB // TN_PROJ)
    vmem = 2 * (_nbytes((TM_PROJ, d), F32) + _nbytes((d, TN_PROJ), BF16) + _nbytes((TM_PROJ, TN_PROJ), BF16))
    vmem += _nbytes((TM_PROJ, d), BF16) + 3 * _nbytes((TM_PROJ, d), F32) // 2 + 2 * _nbytes((TM_PROJ, TN_PROJ), F32)
    kern = functools.partial(_in_proj_kernel, norm_lo=norm_lo, norm_hi=norm_hi)
    return pl.pallas_call(
        kern,
        out_shape=jax.ShapeDtypeStruct((b, s, n), BF16),
        grid=(b, s // TM_PROJ, n // TN_PROJ),
        in_specs=[
            pl.BlockSpec((1, TM_PROJ, d), lambda bi, i, j: (bi, i, 0)),
            pl.BlockSpec((1, 1, 1, d), lambda bi, i, j: (bi, 0, 0, 0)),
            pl.BlockSpec((1, 1, 1, d), lambda bi, i, j: (bi, 1, 0, 0)),
            pl.BlockSpec((1, d), lambda bi, i, j: (0, 0)),
            pl.BlockSpec((d, TN_PROJ), lambda bi, i, j: (0, j)),
            pl.BlockSpec((1, TN_PROJ), lambda bi, i, j: (0, j)),
            pl.BlockSpec((1, TN_PROJ), lambda bi, i, j: (0, j)),
        ],
        out_specs=pl.BlockSpec((1, TM_PROJ, TN_PROJ), lambda bi, i, j: (bi, i, j)),
        scratch_shapes=[pltpu.VMEM((TM_PROJ, d), BF16)],
        compiler_params=_compiler_params(("arbitrary", "arbitrary", "arbitrary"), vmem),
        name="in_proj",
    )(x, mod, mod, ln, w_in, gains, flags)


def _attn_a_kernel(sink_ref, q_ref, kp_ref, km_ref, kn_ref, vp_ref, vm_ref, vn_ref, bias_ref,
                   o_ref):
    hk = pl.program_id(1)
    n = pl.program_id(2)
    last = pl.num_programs(2) - 1
    kcat = jnp.concatenate([kp_ref[0], km_ref[0], kn_ref[0]], axis=0)
    vcat = jnp.concatenate([vp_ref[0], vm_ref[0], vn_ref[0]], axis=0)
    bias = bias_ref[0]
    sink = jnp.concatenate(
        [jnp.full((1, A_BLOCK), sink_ref[hk * A_GROUP + g], F32) for g in range(A_GROUP)], axis=1)
    nt = TQ_A // A_BLOCK
    for t in range(nt):
        q = q_ref[0, t * A_BLOCK:(t + 1) * A_BLOCK, :]
        qs = jnp.concatenate([q[:, g * HEAD_DIM:(g + 1) * HEAD_DIM] for g in range(A_GROUP)], axis=0)
        kw = kcat[t * A_BLOCK:(t + 3) * A_BLOCK]
        vw = vcat[t * A_BLOCK:(t + 3) * A_BLOCK]
        s = lax.dot_general(kw, qs, (((1,), (1,)), ((), ())), preferred_element_type=F32)
        s = s * SCALE + bias
        if t == 0:
            s = jnp.concatenate([jnp.where(n > 0, s[:A_BLOCK], NEG), s[A_BLOCK:]], axis=0)
        if t == nt - 1:
            s = jnp.concatenate([s[:2 * A_BLOCK], jnp.where(n < last, s[2 * A_BLOCK:], NEG)], axis=0)
        m = jnp.maximum(jnp.max(s, axis=0, keepdims=True), sink)
        p = jnp.exp(s - m)
        den = jnp.sum(p, axis=0, keepdims=True) + jnp.exp(sink - m)
        ot = lax.dot_general(vw, p.astype(BF16), (((0,), (0,)), ((), ())),
                             preferred_element_type=F32)
        o = (ot / den).T
        for g in range(A_GROUP):
            o_ref[0, t * A_BLOCK:(t + 1) * A_BLOCK, g * HEAD_DIM:(g + 1) * HEAD_DIM] = (
                o[g * A_BLOCK:(g + 1) * A_BLOCK].astype(BF16))


def _attn_a(z, sink, bias):
    b, s, _ = z.shape
    nblk = s // A_BLOCK
    per = TQ_A // A_BLOCK
    qw = A_GROUP * HEAD_DIM
    q0, k0, v0 = Z_QA // qw, Z_KA // HEAD_DIM, Z_VA // HEAD_DIM

    def prev_map(col0):
        return lambda bi, h, n: (bi, jnp.maximum(n * per - 1, 0), col0 + h)

    def main_map(col0):
        return lambda bi, h, n: (bi, n, col0 + h)

    def next_map(col0):
        return lambda bi, h, n: (bi, jnp.minimum((n + 1) * per, nblk - 1), col0 + h)

    edge = (1, A_BLOCK, HEAD_DIM)
    main = (1, TQ_A, HEAD_DIM)
    vmem = 2 * (2 * _nbytes((TQ_A, qw), BF16) + 2 * _nbytes((TQ_A + 2 * A_BLOCK, HEAD_DIM), BF16)
                + _nbytes((qw, 3 * A_BLOCK), F32)) + 8 * _nbytes((qw, 3 * A_BLOCK), F32) + (4 << 20)
    return pl.pallas_call(
        _attn_a_kernel,
        out_shape=jax.ShapeDtypeStruct((b, s, A_HEADS * HEAD_DIM), BF16),
        grid=(b, A_KV_HEADS, s // TQ_A),
        in_specs=[
            pl.BlockSpec(memory_space=pltpu.SMEM),
            pl.BlockSpec((1, TQ_A, qw), lambda bi, h, n: (bi, n, q0 + h)),
            pl.BlockSpec(edge, prev_map(k0)),
            pl.BlockSpec(main, main_map(k0)),
            pl.BlockSpec(edge, next_map(k0)),
            pl.BlockSpec(edge, prev_map(v0)),
            pl.BlockSpec(main, main_map(v0)),
            pl.BlockSpec(edge, next_map(v0)),
            pl.BlockSpec((1, 3 * A_BLOCK, qw), lambda bi, h, n: (h, 0, 0)),
        ],
        out_specs=pl.BlockSpec((1, TQ_A, qw), lambda bi, h, n: (bi, n, h)),
        compiler_params=_compiler_params(("arbitrary", "arbitrary", "arbitrary"), vmem),
        name="attn_a",
    )(sink, z, z, z, z, z, z, z, bias)


def _alibi_window_bias():
    i = np.arange(A_BLOCK)[None, :]
    j = np.arange(3 * A_BLOCK)[:, None]
    rel = np.abs(A_BLOCK + i - j).astype(np.float32)
    slopes = 2.0 ** (-8.0 * np.arange(1, A_HEADS + 1, dtype=np.float32) / A_HEADS)
    bias = np.where(rel[None] <= A_BLOCK, -slopes[:, None, None] * rel[None], np.float32(NEG))
    bias = bias.astype(np.float32).reshape(A_KV_HEADS, A_GROUP, 3 * A_BLOCK, A_BLOCK)
    bias = bias.transpose(0, 2, 1, 3).reshape(A_KV_HEADS, 3 * A_BLOCK, A_GROUP * A_BLOCK)
    return jnp.asarray(bias)


def _attn_b_kernel(q_ref, k_ref, v_ref, bias_ref, o_ref, *, rows, unroll):
    kh = NB_ROWS
    span = (kh + 1) * GRID_W

    def one_pair(i):
        r = 2 * i
        start = jnp.minimum(jnp.clip(r - kh // 2, 0, rows - kh), rows - kh - 1)
        qs = pl.multiple_of(r * GRID_W, 2 * GRID_W)
        ks = pl.multiple_of(start * GRID_W, GRID_W)
        q = q_ref[0, pl.ds(qs, 2 * GRID_W), :]
        kw = k_ref[0, pl.ds(ks, span), :]
        vw = v_ref[0, pl.ds(ks, span), :]
        s = lax.dot_general(kw, q, (((1,), (1,)), ((), ())), preferred_element_type=F32)
        s = s * SCALE + bias_ref[0, r - start]
        m = jnp.max(s, axis=0, keepdims=True)
        p = jnp.exp(s - m)
        den = jnp.sum(p, axis=0, keepdims=True)
        ot = lax.dot_general(vw, p.astype(BF16), (((0,), (0,)), ((), ())),
                             preferred_element_type=F32)
        o_ref[0, pl.ds(qs, 2 * GRID_W), :] = (ot / den).T.astype(BF16)

    def body(i, carry):
        for u in range(unroll):
            one_pair(i * unroll + u)
        return carry

    lax.fori_loop(0, rows // (2 * unroll), body, 0)


def _attn_b(z, bias):
    b, s, _ = z.shape
    rows = s // GRID_W
    unroll = 4
    assert rows > NB_ROWS and rows % (2 * unroll) == 0
    q0, k0, v0 = Z_QB // HEAD_DIM, Z_KB // HEAD_DIM, Z_VB // HEAD_DIM
    blk = (1, s, HEAD_DIM)
    vmem = 2 * (4 * _nbytes((s, HEAD_DIM), BF16) + _nbytes(bias.shape[1:], F32)) + (8 << 20)
    kern = functools.partial(_attn_b_kernel, rows=rows, unroll=unroll)
    return pl.pallas_call(
        kern,
        out_shape=jax.ShapeDtypeStruct((b, s, B_HEADS * HEAD_DIM), BF16),
        grid=(B_HEADS, b),
        in_specs=[
            pl.BlockSpec(blk, lambda h, bi: (bi, 0, q0 + h)),
            pl.BlockSpec(blk, lambda h, bi: (bi, 0, k0 + h)),
            pl.BlockSpec(blk, lambda h, bi: (bi, 0, v0 + h)),
            pl.BlockSpec((1,) + bias.shape[1:], lambda h, bi: (h, 0, 0, 0)),
        ],
        out_specs=pl.BlockSpec(blk, lambda h, bi: (bi, 0, h)),
        compiler_params=_compiler_params(("arbitrary", "arbitrary"), vmem),
        name="attn_b",
    )(z, z, z, bias)


def _neighbourhood_bias(rpb):
    kh = NB_ROWS
    c = np.arange(GRID_W)
    cw = np.clip(c - NB_COLS // 2, 0, GRID_W - NB_COLS)
    col_ok = (c[None, :] >= cw[:, None]) & (c[None, :] < cw[:, None] + NB_COLS)
    dc = np.clip(c[None, :] - c[:, None], -(NB_COLS - 1), NB_COLS - 1) + (NB_COLS - 1)
    onehot = (dc[None] == np.arange(2 * NB_COLS - 1)[:, None, None]).astype(np.float32)
    cols = jnp.sum(rpb.astype(F32)[:, :, :, None, None] * onehot[None, None], axis=2)
    crow = np.arange(kh)[:, None, None] + np.arange(2)[None, :, None]
    j = np.arange(kh + 1)[None, None, :]
    off = (crow > kh // 2).astype(np.int64)
    row_ok = (j >= off) & (j < off + kh)
    dr = np.clip(j - crow + (kh - 1), 0, 2 * kh - 2)
    t = jnp.stack([cols[:, int(v)] for v in dr.reshape(-1)], axis=1)
    t = t.reshape(rpb.shape[0], kh, 2, kh + 1, GRID_W, GRID_W)
    ok = row_ok[:, :, :, None, None] & col_ok[None, None, None, :, :]
    t = jnp.where(ok[None], t, NEG)
    t = jnp.transpose(t, (0, 1, 3, 5, 2, 4))
    return t.reshape(rpb.shape[0], kh, (kh + 1) * GRID_W, 2 * GRID_W)


def _mix_kernel(oa_ref, ob_ref, ga_ref, gb_ref, wa_ref, wb_ref, m_ref):
    pa = jnp.dot(oa_ref[0], wa_ref[...], preferred_element_type=F32)
    pb = jnp.dot(ob_ref[0], wb_ref[...], preferred_element_type=F32)
    ga = _sigmoid(ga_ref[0].astype(F32))
    gb = _sigmoid(gb_ref[0].astype(F32))
    m_ref[0] = (ga * pa + gb * pb).astype(BF16)


def _mix(oa, ob, z, w_br_a, w_br_b):
    b, s, wa = oa.shape
    d = w_br_a.shape[1]
    assert Z_GA % d == 0 and Z_GB % d == 0
    tm = TM_MIX
    vmem = 2 * (2 * _nbytes((tm, wa), BF16) + 3 * _nbytes((tm, d), BF16) + 2 * _nbytes((wa, d), BF16))
    vmem += 5 * _nbytes((tm, d), F32)
    return pl.pallas_call(
        _mix_kernel,
        out_shape=jax.ShapeDtypeStruct((b, s, d), BF16),
        grid=(b, s // tm),
        in_specs=[
            pl.BlockSpec((1, tm, wa), lambda bi, i: (bi, i, 0)),
            pl.BlockSpec((1, tm, wa), lambda bi, i: (bi, i, 0)),
            pl.BlockSpec((1, tm, d), lambda bi, i: (bi, i, Z_GA // d)),
            pl.BlockSpec((1, tm, d), lambda bi, i: (bi, i, Z_GB // d)),
            pl.BlockSpec((wa, d), lambda bi, i: (0, 0)),
            pl.BlockSpec((wa, d), lambda bi, i: (0, 0)),
        ],
        out_specs=pl.BlockSpec((1, tm, d), lambda bi, i: (bi, i, 0)),
        compiler_params=_compiler_params(("arbitrary", "arbitrary"), vmem),
        name="mix",
    )(oa, ob, z, z, w_br_a, w_br_b)


def _out_proj_kernel(m_ref, x_ref, g_ref, w_ref, o_ref):
    y = jnp.dot(m_ref[0], w_ref[...], preferred_element_type=F32)
    o_ref[0] = x_ref[0] + g_ref[0, 0] * y


def _out_proj(m, x, mod, w_out):
    b, s, d = x.shape
    tm = TM_MIX
    vmem = 2 * (_nbytes((tm, d), BF16) + 2 * _nbytes((tm, d), F32) + _nbytes((d, d), BF16))
    vmem += 2 * _nbytes((tm, d), F32)
    return pl.pallas_call(
        _out_proj_kernel,
        out_shape=jax.ShapeDtypeStruct((b, s, d), F32),
        grid=(b, s // tm),
        in_specs=[
            pl.BlockSpec((1, tm, d), lambda bi, i: (bi, i, 0)),
            pl.BlockSpec((1, tm, d), lambda bi, i: (bi, i, 0)),
            pl.BlockSpec((1, 1, 1, d), lambda bi, i: (bi, 2, 0, 0)),
            pl.BlockSpec((d, d), lambda bi, i: (0, 0)),
        ],
        out_specs=pl.BlockSpec((1, tm, d), lambda bi, i: (bi, i, 0)),
        compiler_params=_compiler_params(("arbitrary", "arbitrary"), vmem),
        name="out_proj",
    )(m, x, mod, w_out)


def _mlp_kernel(x_ref, sh_ref, sc_ref, g_ref, ln_ref, w1_ref, w2_ref, o_ref, h_scr):
    f = pl.program_id(2)

    @pl.when(f == 0)
    def _():
        h = _modulated_rmsnorm(x_ref[0], ln_ref[...], sh_ref[0, 0], sc_ref[0, 0])
        h_scr[...] = h.astype(BF16)
        o_ref[...] = jnp.zeros_like(o_ref)

    u = jnp.dot(h_scr[...], w1_ref[...], preferred_element_type=F32)
    a = jnp.square(jnp.maximum(u, 0.0)).astype(BF16)
    o_ref[0] += jnp.dot(a, w2_ref[...], preferred_element_type=F32)

    @pl.when(f == pl.num_programs(2) - 1)
    def _():
        o_ref[0] = x_ref[0] + g_ref[0, 0] * o_ref[0]


def _mlp(x, mod, ln, w1, w2):
    b, s, d = x.shape
    ff = w1.shape[1]
    tm, tf = TM_MLP, TF_MLP
    vmem = 2 * (2 * _nbytes((tm, d), F32) + 2 * _nbytes((d, tf), BF16)) + _nbytes((tm, d), BF16)
    vmem += 2 * _nbytes((tm, tf), F32) + 2 * _nbytes((tm, d), F32)
    return pl.pallas_call(
        _mlp_kernel,
        out_shape=jax.ShapeDtypeStruct((b, s, d), F32),
        grid=(b, s // tm, ff // tf),
        in_specs=[
            pl.BlockSpec((1, tm, d), lambda bi, i, f: (bi, i, 0)),
            pl.BlockSpec((1, 1, 1, d), lambda bi, i, f: (bi, 3, 0, 0)),
            pl.BlockSpec((1, 1, 1, d), lambda bi, i, f: (bi, 4, 0, 0)),
            pl.BlockSpec((1, 1, 1, d), lambda bi, i, f: (bi, 5, 0, 0)),
            pl.BlockSpec((1, d), lambda bi, i, f: (0, 0)),
            pl.BlockSpec((d, tf), lambda bi, i, f: (0, f)),
            pl.BlockSpec((tf, d), lambda bi, i, f: (f, 0)),
        ],
        out_specs=pl.BlockSpec((1, tm, d), lambda bi, i, f: (bi, i, 0)),
        scratch_shapes=[pltpu.VMEM((tm, d), BF16)],
        compiler_params=_compiler_params(("arbitrary", "arbitrary", "arbitrary"), vmem),
        name="mlp",
    )(x, mod, mod, mod, ln, w1, w2)


def _split_w_in(w_in_l):
    aq, akv, bw = A_HEADS * HEAD_DIM, A_KV_HEADS * HEAD_DIM, B_HEADS * HEAD_DIM
    cuts = np.cumsum([aq, akv, akv, bw, bw, bw, D_MODEL])
    qa, ka, va, qb, kb, vb, ga, gb = jnp.split(w_in_l, cuts, axis=1)
    return jnp.concatenate([ga, gb, qa, ka, va, qb, kb, vb], axis=1).astype(BF16)


def _qk_norm_vectors(qn_a, kn_a, qn_b, kn_b):
    ones = jnp.ones((HEAD_DIM,), F32)
    chunks, flags = [], []
    for gain, count, on in (
        (ones, 2 * D_MODEL // HEAD_DIM, 0.0),
        (qn_a, A_HEADS, 1.0), (kn_a, A_KV_HEADS, 1.0), (ones, A_KV_HEADS, 0.0),
        (qn_b, B_HEADS, 1.0), (kn_b, B_HEADS, 1.0), (ones, B_HEADS, 0.0),
    ):
        chunks.append(jnp.tile(gain.astype(F32), count))
        flags.append(jnp.full((count * HEAD_DIM,), on, F32))
    return jnp.concatenate(chunks)[None, :], jnp.concatenate(flags)[None, :]


def kernel(x_prompt, x_sample, c_prompt, c_sample, w_ada, b_ada, ln1, ln2, w_in, qn_a, kn_a, qn_b,
           kn_b, sink_a, rpb_b, w_br_a, w_br_b, w_out, w_mlp1, w_mlp2):
    depth = w_ada.shape[0]
    nb_p = x_prompt.shape[0]
    c = jnp.concatenate([c_prompt, c_sample], axis=0)
    mod = _adaln(c, w_ada, b_ada)
    mod = mod.reshape(depth, c.shape[0], 6, 1, D_MODEL)
    alibi = _alibi_window_bias()
    xs = [x_prompt, x_sample]
    for l in range(depth):
        w_in_l = _split_w_in(w_in[l])
        gains, flags = _qk_norm_vectors(qn_a[l], kn_a[l], qn_b[l], kn_b[l])
        nbias = _neighbourhood_bias(rpb_b[l])
        wa, wb = w_br_a[l].astype(BF16), w_br_b[l].astype(BF16)
        wo, w1, w2 = w_out[l].astype(BF16), w_mlp1[l].astype(BF16), w_mlp2[l].astype(BF16)
        ln1_l, ln2_l = ln1[l][None, :], ln2[l][None, :]
        sink_l = sink_a[l].astype(F32)
        for g in range(2):
            x = xs[g]
            mod_g = mod[l, :nb_p] if g == 0 else mod[l, nb_p:]
            z = _in_proj(x, mod_g, ln1_l, w_in_l, gains, flags)
            oa = _attn_a(z, sink_l, alibi)
            ob = _attn_b(z, nbias)
            m = _mix(oa, ob, z, wa, wb)
            x = _out_proj(m, x, mod_g, wo)
            xs[g] = _mlp(x, mod_g, ln2_l, w1, w2)
    return (xs[0], xs[1])
```

```python
import functools
import math

import jax
import jax.numpy as jnp
import numpy as np
from jax import lax
from jax.experimental import pallas as pl
from jax.experimental.pallas import tpu as pltpu

F32 = jnp.float32
BF16 = jnp.bfloat16

D_MODEL = 2048
HEAD_DIM = 128
A_HEADS = 8
A_KV_HEADS = 2
A_GROUP = A_HEADS // A_KV_HEADS
A_BLOCK = 128
B_HEADS = 8
GRID_W = 64
NB_ROWS = 8
NB_COLS = 16
D_FF = 4 * D_MODEL
EPS = 1e-6
NEG = -1e30
SCALE = HEAD_DIM ** -0.5

Z_GA = 0
Z_GB = Z_GA + D_MODEL
Z_QA = Z_GB + D_MODEL
Z_KA = Z_QA + A_HEADS * HEAD_DIM
Z_VA = Z_KA + A_KV_HEADS * HEAD_DIM
Z_QB = Z_VA + A_KV_HEADS * HEAD_DIM
Z_KB = Z_QB + B_HEADS * HEAD_DIM
Z_VB = Z_KB + B_HEADS * HEAD_DIM
Z_COLS = Z_VB + B_HEADS * HEAD_DIM

LANES = 128
V7X_VMEM_BUDGET_BYTES = 56 * 1024 * 1024

TM_PROJ = 1024
TN_PROJ = 512
TM_MIX = 512
TM_MLP = 512
TF_MLP = 1024
TQ_A = 1024
TN_ADA = 1024


def _compiler_params(semantics, vmem_bytes):
    limit = int(min(max(vmem_bytes, 16 * 1024 * 1024), V7X_VMEM_BUDGET_BYTES))
    return pltpu.CompilerParams(dimension_semantics=semantics, vmem_limit_bytes=limit)


def _nbytes(shape, dtype):
    return int(np.prod(shape)) * jnp.dtype(dtype).itemsize


def _sigmoid(x):
    return 1.0 / (1.0 + jnp.exp(-x))


def _adaln_kernel(c_ref, w_ref, b_ref, o_ref):
    c = c_ref[...]
    a = (c * _sigmoid(c)).astype(BF16)
    w = w_ref[0].astype(BF16)
    o_ref[0] = jnp.dot(a, w, preferred_element_type=F32) + b_ref[0]


def _adaln(c, w_ada, b_ada):
    depth, d, n = w_ada.shape
    nb = c.shape[0]
    vmem = 2 * (_nbytes((d, TN_ADA), F32) + _nbytes((nb, d), F32) + _nbytes((nb, TN_ADA), F32))
    vmem += _nbytes((d, TN_ADA), BF16) + (4 << 20)
    return pl.pallas_call(
        _adaln_kernel,
        out_shape=jax.ShapeDtypeStruct((depth, nb, n), F32),
        grid=(depth, n // TN_ADA),
        in_specs=[
            pl.BlockSpec((nb, d), lambda l, j: (0, 0)),
            pl.BlockSpec((1, d, TN_ADA), lambda l, j: (l, 0, j)),
            pl.BlockSpec((1, 1, TN_ADA), lambda l, j: (l, 0, j)),
        ],
        out_specs=pl.BlockSpec((1, nb, TN_ADA), lambda l, j: (l, 0, j)),
        compiler_params=_compiler_params(("arbitrary", "arbitrary"), vmem),
        name="adaln",
    )(c, w_ada, b_ada.reshape(depth, 1, n))


def _modulated_rmsnorm(x, ln, shift, scale):
    ms = jnp.mean(x * x, axis=-1, keepdims=True)
    y = x * lax.rsqrt(ms + EPS)
    return y * (ln * (1.0 + scale)) + shift


def _in_proj_kernel(x_ref, sh_ref, sc_ref, ln_ref, w_ref, g_ref, f_ref, z_ref, h_scr, *,
                    norm_lo, norm_hi):
    j = pl.program_id(2)

    @pl.when(j == 0)
    def _():
        h = _modulated_rmsnorm(x_ref[0], ln_ref[...], sh_ref[0, 0], sc_ref[0, 0])
        h_scr[...] = h.astype(BF16)

    acc = jnp.dot(h_scr[...], w_ref[...], preferred_element_type=F32)
    is_norm_tile = jnp.logical_and(j >= norm_lo, j < norm_hi)

    @pl.when(is_norm_tile)
    def _():
        for c in range(TN_PROJ // LANES):
            sl = slice(c * LANES, (c + 1) * LANES)
            blk = acc[:, sl]
            r = lax.rsqrt(jnp.mean(blk * blk, axis=-1, keepdims=True) + EPS)
            fac = jnp.where(f_ref[:, sl] > 0.0, r * g_ref[:, sl], 1.0)
            z_ref[0, :, sl] = (blk * fac).astype(BF16)

    @pl.when(jnp.logical_not(is_norm_tile))
    def _():
        z_ref[0] = acc.astype(BF16)


def _in_proj(x, mod, ln, w_in, gains, flags):
    b, s, d = x.shape
    n = w_in.shape[1]
    assert s % TM_PROJ == 0 and n % TN_PROJ == 0
    assert Z_QA % TN_PROJ == 0
    norm_lo = Z_QA // TN_PROJ
    norm_hi = -(-Z_VB // TN_PROJ)
    vmem = 2 * (_nbytes((TM_PROJ, d), F32) + _nbytes((d, TN_PROJ), BF16) + _nbytes((TM_PROJ, TN_PROJ), BF16))
    vmem += _nbytes((TM_PROJ, d), BF16) + 3 * _nbytes((TM_PROJ, d), F32) // 2 + 2 * _nbytes((TM_PROJ, TN_PROJ), F32)
    kern = functools.partial(_in_proj_kernel, norm_lo=norm_lo, norm_hi=norm_hi)
    return pl.pallas_call(
        kern,
        out_shape=jax.ShapeDtypeStruct((b, s, n), BF16),
        grid=(b, s // TM_PROJ, n // TN_PROJ),
        in_specs=[
            pl.BlockSpec((1, TM_PROJ, d), lambda bi, i, j: (bi, i, 0)),
            pl.BlockSpec((1, 1, 1, d), lambda bi, i, j: (bi, 0, 0, 0)),
            pl.BlockSpec((1, 1, 1, d), lambda bi, i, j: (bi, 1, 0, 0)),
            pl.BlockSpec((1, d), lambda bi, i, j: (0, 0)),
            pl.BlockSpec((d, TN_PROJ), lambda bi, i, j: (0, j)),
            pl.BlockSpec((1, TN_PROJ), lambda bi, i, j: (0, j)),
            pl.BlockSpec((1, TN_PROJ), lambda bi, i, j: (0, j)),
        ],
        out_specs=pl.BlockSpec((1, TM_PROJ, TN_PROJ), lambda bi, i, j: (bi, i, j)),
        scratch_shapes=[pltpu.VMEM((TM_PROJ, d), BF16)],
        compiler_params=_compiler_params(("arbitrary", "arbitrary", "arbitrary"), vmem),
        name="in_proj",
    )(x, mod, mod, ln, w_in, gains, flags)


def _attn_a_kernel(sink_ref, q_ref, kp_ref, km_ref, kn_ref, vp_ref, vm_ref, vn_ref, bias_ref,
                   o_ref):
    hk = pl.program_id(1)
    n = pl.program_id(2)
    last = pl.num_programs(2) - 1
    kcat = jnp.concatenate([kp_ref[0], km_ref[0], kn_ref[0]], axis=0)
    vcat = jnp.concatenate([vp_ref[0], vm_ref[0], vn_ref[0]], axis=0)
    bias = bias_ref[0]
    sink = jnp.concatenate(
        [jnp.full((1, A_BLOCK), sink_ref[hk * A_GROUP + g], F32) for g in range(A_GROUP)], axis=1)
    nt = TQ_A // A_BLOCK
    for t in range(nt):
        q = q_ref[0, t * A_BLOCK:(t + 1) * A_BLOCK, :]
        qs = jnp.concatenate([q[:, g * HEAD_DIM:(g + 1) * HEAD_DIM] for g in range(A_GROUP)], axis=0)
        kw = kcat[t * A_BLOCK:(t + 3) * A_BLOCK]
        vw = vcat[t * A_BLOCK:(t + 3) * A_BLOCK]
        s = lax.dot_general(kw, qs, (((1,), (1,)), ((), ())), preferred_element_type=F32)
        s = s * SCALE + bias
        if t == 0:
            s = jnp.concatenate([jnp.where(n > 0, s[:A_BLOCK], NEG), s[A_BLOCK:]], axis=0)
        if t == nt - 1:
            s = jnp.concatenate([s[:2 * A_BLOCK], jnp.where(n < last, s[2 * A_BLOCK:], NEG)], axis=0)
        m = jnp.maximum(jnp.max(s, axis=0, keepdims=True), sink)
        p = jnp.exp(s - m)
        den = jnp.sum(p, axis=0, keepdims=True) + jnp.exp(sink - m)
        ot = lax.dot_general(vw, p.astype(BF16), (((0,), (0,)), ((), ())),
                             preferred_element_type=F32)
        o = (ot / den).T
        for g in range(A_GROUP):
            o_ref[0, t * A_BLOCK:(t + 1) * A_BLOCK, g * HEAD_DIM:(g + 1) * HEAD_DIM] = (
                o[g * A_BLOCK:(g + 1) * A_BLOCK].astype(BF16))


def _attn_a(z, sink, bias):
    b, s, _ = z.shape
    nblk = s // A_BLOCK
    per = TQ_A // A_BLOCK
    qw = A_GROUP * HEAD_DIM
    q0, k0, v0 = Z_QA // qw, Z_KA // HEAD_DIM, Z_VA // HEAD_DIM

    def prev_map(col0):
        return lambda bi, h, n: (bi, jnp.maximum(n * per - 1, 0), col0 + h)

    def main_map(col0):
        return lambda bi, h, n: (bi, n, col0 + h)

    def next_map(col0):
        return lambda bi, h, n: (bi, jnp.minimum((n + 1) * per, nblk - 1), col0 + h)

    edge = (1, A_BLOCK, HEAD_DIM)
    main = (1, TQ_A, HEAD_DIM)
    vmem = 2 * (2 * _nbytes((TQ_A, qw), BF16) + 2 * _nbytes((TQ_A + 2 * A_BLOCK, HEAD_DIM), BF16)
                + _nbytes((qw, 3 * A_BLOCK), F32)) + 8 * _nbytes((qw, 3 * A_BLOCK), F32) + (4 << 20)
    return pl.pallas_call(
        _attn_a_kernel,
        out_shape=jax.ShapeDtypeStruct((b, s, A_HEADS * HEAD_DIM), BF16),
        grid=(b, A_KV_HEADS, s // TQ_A),
        in_specs=[
            pl.BlockSpec(memory_space=pltpu.SMEM),
            pl.BlockSpec((1, TQ_A, qw), lambda bi, h, n: (bi, n, q0 + h)),
            pl.BlockSpec(edge, prev_map(k0)),
            pl.BlockSpec(main, main_map(k0)),
            pl.BlockSpec(edge, next_map(k0)),
            pl.BlockSpec(edge, prev_map(v0)),
            pl.BlockSpec(main, main_map(v0)),
            pl.BlockSpec(edge, next_map(v0)),
            pl.BlockSpec((1, 3 * A_BLOCK, qw), lambda bi, h, n: (h, 0, 0)),
        ],
        out_specs=pl.BlockSpec((1, TQ_A, qw), lambda bi, h, n: (bi, n, h)),
        compiler_params=_compiler_params(("arbitrary", "arbitrary", "arbitrary"), vmem),
        name="attn_a",
    )(sink, z, z, z, z, z, z, z, bias)


def _alibi_window_bias():
    i = np.arange(A_BLOCK)[None, :]
    j = np.arange(3 * A_BLOCK)[:, None]
    rel = np.abs(A_BLOCK + i - j).astype(np.float32)
    slopes = 2.0 ** (-8.0 * np.arange(1, A_HEADS + 1, dtype=np.float32) / A_HEADS)
    bias = np.where(rel[None] <= A_BLOCK, -slopes[:, None, None] * rel[None], np.float32(NEG))
    bias = bias.astype(np.float32).reshape(A_KV_HEADS, A_GROUP, 3 * A_BLOCK, A_BLOCK)
    bias = bias.transpose(0, 2, 1, 3).reshape(A_KV_HEADS, 3 * A_BLOCK, A_GROUP * A_BLOCK)
    return jnp.asarray(bias)


def _attn_b_kernel(q_ref, k_ref, v_ref, bias_ref, o_ref, *, rows, unroll):
    kh = NB_ROWS
    span = (kh + 1) * GRID_W

    def one_pair(i):
        r = 2 * i
        start = jnp.minimum(jnp.clip(r - kh // 2, 0, rows - kh), rows - kh - 1)
        qs = pl.multiple_of(r * GRID_W, 2 * GRID_W)
        ks = pl.multiple_of(start * GRID_W, GRID_W)
        q = q_ref[0, pl.ds(qs, 2 * GRID_W), :]
        kw = k_ref[0, pl.ds(ks, span), :]
        vw = v_ref[0, pl.ds(ks, span), :]
        s = lax.dot_general(kw, q, (((1,), (1,)), ((), ())), preferred_element_type=F32)
        s = s * SCALE + bias_ref[0, r - start]
        m = jnp.max(s, axis=0, keepdims=True)
        p = jnp.exp(s - m)
        den = jnp.sum(p, axis=0, keepdims=True)
        ot = lax.dot_general(vw, p.astype(BF16), (((0,), (0,)), ((), ())),
                             preferred_element_type=F32)
        o_ref[0, pl.ds(qs, 2 * GRID_W), :] = (ot / den).T.astype(BF16)

    def body(i, carry):
        for u in range(unroll):
            one_pair(i * unroll + u)
        return carry

    lax.fori_loop(0, rows // (2 * unroll), body, 0)


def _attn_b(z, bias):
    b, s, _ = z.shape
    rows = s // GRID_W
    unroll = 4
    assert rows > NB_ROWS and rows % (2 * unroll) == 0
    q0, k0, v0 = Z_QB // HEAD_DIM, Z_KB // HEAD_DIM, Z_VB // HEAD_DIM
    blk = (1, s, HEAD_DIM)
    vmem = 2 * (4 * _nbytes((s, HEAD_DIM), BF16) + _nbytes(bias.shape[1:], F32)) + (8 << 20)
    kern = functools.partial(_attn_b_kernel, rows=rows, unroll=unroll)
    return pl.pallas_call(
        kern,
        out_shape=jax.ShapeDtypeStruct((b, s, B_HEADS * HEAD_DIM), BF16),
        grid=(B_HEADS, b),
        in_specs=[
            pl.BlockSpec(blk, lambda h, bi: (bi, 0, q0 + h)),
            pl.BlockSpec(blk, lambda h, bi: (bi, 0, k0 + h)),
            pl.BlockSpec(blk, lambda h, bi: (bi, 0, v0 + h)),
            pl.BlockSpec((1,) + bias.shape[1:], lambda h, bi: (h, 0, 0, 0)),
        ],
        out_specs=pl.BlockSpec(blk, lambda h, bi: (bi, 0, h)),
        compiler_params=_compiler_params(("arbitrary", "arbitrary"), vmem),
        name="attn_b",
    )(z, z, z, bias)


def _neighbourhood_bias(rpb):
    kh = NB_ROWS
    c = np.arange(GRID_W)
    cw = np.clip(c - NB_COLS // 2, 0, GRID_W - NB_COLS)
    col_ok = (c[None, :] >= cw[:, None]) & (c[None, :] < cw[:, None] + NB_COLS)
    dc = np.clip(c[None, :] - c[:, None], -(NB_COLS - 1), NB_COLS - 1) + (NB_COLS - 1)
    onehot = (dc[None] == np.arange(2 * NB_COLS - 1)[:, None, None]).astype(np.float32)
    cols = jnp.sum(rpb.astype(F32)[:, :, :, None, None] * onehot[None, None], axis=2)
    crow = np.arange(kh)[:, None, None] + np.arange(2)[None, :, None]
    j = np.arange(kh + 1)[None, None, :]
    off = (crow > kh // 2).astype(np.int64)
    row_ok = (j >= off) & (j < off + kh)
    colsp = jnp.pad(cols, ((0, 0), (2, 2), (0, 0), (0, 0)))
    t = jnp.stack([colsp[:, kh + 1 - int(cc):2 * kh + 2 - int(cc)] for cc in crow.reshape(-1)], axis=1)
    t = t.reshape(rpb.shape[0], kh, 2, kh + 1, GRID_W, GRID_W)
    ok = row_ok[:, :, :, None, None] & col_ok[None, None, None, :, :]
    t = jnp.where(ok[None], t, NEG)
    t = jnp.transpose(t, (0, 1, 3, 5, 2, 4))
    return t.reshape(rpb.shape[0], kh, (kh + 1) * GRID_W, 2 * GRID_W)


def _mix_kernel(oa_ref, ob_ref, ga_ref, gb_ref, wa_ref, wb_ref, m_ref):
    pa = jnp.dot(oa_ref[0], wa_ref[...], preferred_element_type=F32)
    pb = jnp.dot(ob_ref[0], wb_ref[...], preferred_element_type=F32)
    ga = _sigmoid(ga_ref[0].astype(F32))
    gb = _sigmoid(gb_ref[0].astype(F32))
    m_ref[0] = (ga * pa + gb * pb).astype(BF16)


def _mix(oa, ob, z, w_br_a, w_br_b):
    b, s, wa = oa.shape
    d = w_br_a.shape[1]
    assert Z_GA % d == 0 and Z_GB % d == 0
    tm = TM_MIX
    vmem = 2 * (2 * _nbytes((tm, wa), BF16) + 3 * _nbytes((tm, d), BF16) + 2 * _nbytes((wa, d), BF16))
    vmem += 5 * _nbytes((tm, d), F32)
    return pl.pallas_call(
        _mix_kernel,
        out_shape=jax.ShapeDtypeStruct((b, s, d), BF16),
        grid=(b, s // tm),
        in_specs=[
            pl.BlockSpec((1, tm, wa), lambda bi, i: (bi, i, 0)),
            pl.BlockSpec((1, tm, wa), lambda bi, i: (bi, i, 0)),
            pl.BlockSpec((1, tm, d), lambda bi, i: (bi, i, Z_GA // d)),
            pl.BlockSpec((1, tm, d), lambda bi, i: (bi, i, Z_GB // d)),
            pl.BlockSpec((wa, d), lambda bi, i: (0, 0)),
            pl.BlockSpec((wa, d), lambda bi, i: (0, 0)),
        ],
        out_specs=pl.BlockSpec((1, tm, d), lambda bi, i: (bi, i, 0)),
        compiler_params=_compiler_params(("arbitrary", "arbitrary"), vmem),
        name="mix",
    )(oa, ob, z, z, w_br_a, w_br_b)


def _out_proj_kernel(m_ref, x_ref, g_ref, w_ref, o_ref):
    y = jnp.dot(m_ref[0], w_ref[...], preferred_element_type=F32)
    o_ref[0] = x_ref[0] + g_ref[0, 0] * y


def _out_proj(m, x, mod, w_out):
    b, s, d = x.shape
    tm = TM_MIX
    vmem = 2 * (_nbytes((tm, d), BF16) + 2 * _nbytes((tm, d), F32) + _nbytes((d, d), BF16))
    vmem += 2 * _nbytes((tm, d), F32)
    return pl.pallas_call(
        _out_proj_kernel,
        out_shape=jax.ShapeDtypeStruct((b, s, d), F32),
        grid=(b, s // tm),
        in_specs=[
            pl.BlockSpec((1, tm, d), lambda bi, i: (bi, i, 0)),
            pl.BlockSpec((1, tm, d), lambda bi, i: (bi, i, 0)),
            pl.BlockSpec((1, 1, 1, d), lambda bi, i: (bi, 2, 0, 0)),
            pl.BlockSpec((d, d), lambda bi, i: (0, 0)),
        ],
        out_specs=pl.BlockSpec((1, tm, d), lambda bi, i: (bi, i, 0)),
        compiler_params=_compiler_params(("arbitrary", "arbitrary"), vmem),
        name="out_proj",
    )(m, x, mod, w_out)


def _mlp_kernel(x_ref, sh_ref, sc_ref, g_ref, ln_ref, w1_ref, w2_ref, o_ref, h_scr):
    f = pl.program_id(2)

    @pl.when(f == 0)
    def _():
        h = _modulated_rmsnorm(x_ref[0], ln_ref[...], sh_ref[0, 0], sc_ref[0, 0])
        h_scr[...] = h.astype(BF16)
        o_ref[...] = jnp.zeros_like(o_ref)

    u = jnp.dot(h_scr[...], w1_ref[...], preferred_element_type=F32)
    a = jnp.square(jnp.maximum(u, 0.0)).astype(BF16)
    o_ref[0] += jnp.dot(a, w2_ref[...], preferred_element_type=F32)

    @pl.when(f == pl.num_programs(2) - 1)
    def _():
        o_ref[0] = x_ref[0] + g_ref[0, 0] * o_ref[0]


def _mlp(x, mod, ln, w1, w2):
    b, s, d = x.shape
    ff = w1.shape[1]
    tm, tf = TM_MLP, TF_MLP
    vmem = 2 * (2 * _nbytes((tm, d), F32) + 2 * _nbytes((d, tf), BF16)) + _nbytes((tm, d), BF16)
    vmem += 2 * _nbytes((tm, tf), F32) + 2 * _nbytes((tm, d), F32)
    return pl.pallas_call(
        _mlp_kernel,
        out_shape=jax.ShapeDtypeStruct((b, s, d), F32),
        grid=(b, s // tm, ff // tf),
        in_specs=[
            pl.BlockSpec((1, tm, d), lambda bi, i, f: (bi, i, 0)),
            pl.BlockSpec((1, 1, 1, d), lambda bi, i, f: (bi, 3, 0, 0)),
            pl.BlockSpec((1, 1, 1, d), lambda bi, i, f: (bi, 4, 0, 0)),
            pl.BlockSpec((1, 1, 1, d), lambda bi, i, f: (bi, 5, 0, 0)),
            pl.BlockSpec((1, d), lambda bi, i, f: (0, 0)),
            pl.BlockSpec((d, tf), lambda bi, i, f: (0, f)),
            pl.BlockSpec((tf, d), lambda bi, i, f: (f, 0)),
        ],
        out_specs=pl.BlockSpec((1, tm, d), lambda bi, i, f: (bi, i, 0)),
        scratch_shapes=[pltpu.VMEM((tm, d), BF16)],
        compiler_params=_compiler_params(("arbitrary", "arbitrary", "arbitrary"), vmem),
        name="mlp",
    )(x, mod, mod, mod, ln, w1, w2)


def _split_w_in(w_in_l):
    aq, akv, bw = A_HEADS * HEAD_DIM, A_KV_HEADS * HEAD_DIM, B_HEADS * HEAD_DIM
    cuts = np.cumsum([aq, akv, akv, bw, bw, bw, D_MODEL])
    qa, ka, va, qb, kb, vb, ga, gb = jnp.split(w_in_l, cuts, axis=1)
    return jnp.concatenate([ga, gb, qa, ka, va, qb, kb, vb], axis=1).astype(BF16)


def _qk_norm_vectors(qn_a, kn_a, qn_b, kn_b):
    ones = jnp.ones((HEAD_DIM,), F32)
    chunks, flags = [], []
    for gain, count, on in (
        (ones, 2 * D_MODEL // HEAD_DIM, 0.0),
        (qn_a, A_HEADS, 1.0), (kn_a, A_KV_HEADS, 1.0), (ones, A_KV_HEADS, 0.0),
        (qn_b, B_HEADS, 1.0), (kn_b, B_HEADS, 1.0), (ones, B_HEADS, 0.0),
    ):
        chunks.append(jnp.tile(gain.astype(F32), count))
        flags.append(jnp.full((count * HEAD_DIM,), on, F32))
    return jnp.concatenate(chunks)[None, :], jnp.concatenate(flags)[None, :]


def kernel(x_prompt, x_sample, c_prompt, c_sample, w_ada, b_ada, ln1, ln2, w_in, qn_a, kn_a, qn_b,
           kn_b, sink_a, rpb_b, w_br_a, w_br_b, w_out, w_mlp1, w_mlp2):
    depth = w_ada.shape[0]
    nb_p = x_prompt.shape[0]
    c = jnp.concatenate([c_prompt, c_sample], axis=0)
    mod = _adaln(c, w_ada, b_ada)
    mod = mod.reshape(depth, c.shape[0], 6, 1, D_MODEL)
    alibi = _alibi_window_bias()
    xs = [x_prompt, x_sample]
    for l in range(depth):
        w_in_l = _split_w_in(w_in[l])
        gains, flags = _qk_norm_vectors(qn_a[l], kn_a[l], qn_b[l], kn_b[l])
        nbias = _neighbourhood_bias(rpb_b[l])
        wa, wb = w_br_a[l].astype(BF16), w_br_b[l].astype(BF16)
        wo, w1, w2 = w_out[l].astype(BF16), w_mlp1[l].astype(BF16), w_mlp2[l].astype(BF16)
        ln1_l, ln2_l = ln1[l][None, :], ln2[l][None, :]
        sink_l = sink_a[l].astype(F32)
        for g in range(2):
            x = xs[g]
            mod_g = mod[l, :nb_p] if g == 0 else mod[l, nb_p:]
            z = _in_proj(x, mod_g, ln1_l, w_in_l, gains, flags)
            oa = _attn_a(z, sink_l, alibi)
            ob = _attn_b(z, nbias)
            m = _mix(oa, ob, z, wa, wb)
            x = _out_proj(m, x, mod_g, wo)
            xs[g] = _mlp(x, mod_g, ln2_l, w1, w2)
    return (xs[0], xs[1])
```

```python
import functools
import math

import jax
import jax.numpy as jnp
import numpy as np
from jax import lax
from jax.experimental import pallas as pl
from jax.experimental.pallas import tpu as pltpu

F32 = jnp.float32
BF16 = jnp.bfloat16

D_MODEL = 2048
HEAD_DIM = 128
A_HEADS = 8
A_KV_HEADS = 2
A_GROUP = A_HEADS // A_KV_HEADS
A_BLOCK = 128
B_HEADS = 8
GRID_W = 64
NB_ROWS = 8
NB_COLS = 16
D_FF = 4 * D_MODEL
EPS = 1e-6
NEG = -1e30
SCALE = HEAD_DIM ** -0.5

Z_GA = 0
Z_GB = Z_GA + D_MODEL
Z_QA = Z_GB + D_MODEL
Z_KA = Z_QA + A_HEADS * HEAD_DIM
Z_VA = Z_KA + A_KV_HEADS * HEAD_DIM
Z_QB = Z_VA + A_KV_HEADS * HEAD_DIM
Z_KB = Z_QB + B_HEADS * HEAD_DIM
Z_VB = Z_KB + B_HEADS * HEAD_DIM
Z_COLS = Z_VB + B_HEADS * HEAD_DIM

LANES = 128
V7X_VMEM_BUDGET_BYTES = 56 * 1024 * 1024

TM_PROJ = 1024
TN_PROJ = 512
TM_MIX = 512
TM_MLP = 512
TF_MLP = 1024
TQ_A = 1024
TN_ADA = 1024


def _compiler_params(semantics, vmem_bytes):
    limit = int(min(max(vmem_bytes, 16 * 1024 * 1024), V7X_VMEM_BUDGET_BYTES))
    return pltpu.CompilerParams(dimension_semantics=semantics, vmem_limit_bytes=limit)


def _nbytes(shape, dtype):
    return int(np.prod(shape)) * jnp.dtype(dtype).itemsize


def _sigmoid(x):
    return 1.0 / (1.0 + jnp.exp(-x))


def _adaln_kernel(c_ref, w_ref, b_ref, o_ref):
    c = c_ref[...]
    a = (c * _sigmoid(c)).astype(BF16)
    w = w_ref[0].astype(BF16)
    o_ref[0] = jnp.dot(a, w, preferred_element_type=F32) + b_ref[0]


def _adaln(c, w_ada, b_ada):
    depth, d, n = w_ada.shape
    nb = c.shape[0]
    vmem = 2 * (_nbytes((d, TN_ADA), F32) + _nbytes((nb, d), F32) + _nbytes((nb, TN_ADA), F32))
    vmem += _nbytes((d, TN_ADA), BF16) + (4 << 20)
    return pl.pallas_call(
        _adaln_kernel,
        out_shape=jax.ShapeDtypeStruct((depth, nb, n), F32),
        grid=(depth, n // TN_ADA),
        in_specs=[
            pl.BlockSpec((nb, d), lambda l, j: (0, 0)),
            pl.BlockSpec((1, d, TN_ADA), lambda l, j: (l, 0, j)),
            pl.BlockSpec((1, 1, TN_ADA), lambda l, j: (l, 0, j)),
        ],
        out_specs=pl.BlockSpec((1, nb, TN_ADA), lambda l, j: (l, 0, j)),
        compiler_params=_compiler_params(("arbitrary", "arbitrary"), vmem),
        name="adaln",
    )(c, w_ada, b_ada.reshape(depth, 1, n))


def _modulated_rmsnorm(x, ln, shift, scale):
    ms = jnp.mean(x * x, axis=-1, keepdims=True)
    y = x * lax.rsqrt(ms + EPS)
    return y * (ln * (1.0 + scale)) + shift


def _in_proj_kernel(x_ref, sh_ref, sc_ref, ln_ref, w_ref, g_ref, f_ref, z_ref, h_scr, *,
                    norm_lo, norm_hi):
    j = pl.program_id(2)

    @pl.when(j == 0)
    def _():
        h = _modulated_rmsnorm(x_ref[0], ln_ref[...], sh_ref[0, 0], sc_ref[0, 0])
        h_scr[...] = h.astype(BF16)

    acc = jnp.dot(h_scr[...], w_ref[...], preferred_element_type=F32)
    is_norm_tile = jnp.logical_and(j >= norm_lo, j < norm_hi)

    @pl.when(is_norm_tile)
    def _():
        for c in range(TN_PROJ // LANES):
            sl = slice(c * LANES, (c + 1) * LANES)
            blk = acc[:, sl]
            r = lax.rsqrt(jnp.mean(blk * blk, axis=-1, keepdims=True) + EPS)
            fac = jnp.where(f_ref[:, sl] > 0.0, r * g_ref[:, sl], 1.0)
            z_ref[0, :, sl] = (blk * fac).astype(BF16)

    @pl.when(jnp.logical_not(is_norm_tile))
    def _():
        z_ref[0] = acc.astype(BF16)


def _in_proj(x, mod, ln, w_in, gains, flags):
    b, s, d = x.shape
    n = w_in.shape[1]
    assert s % TM_PROJ == 0 and n % TN_PROJ == 0
    assert Z_QA % TN_PROJ == 0
    norm_lo = Z_QA // TN_PROJ
    norm_hi = -(-Z_VB // TN_PROJ)
    vmem = 2 * (_nbytes((TM_PROJ, d), F32) + _nbytes((d, TN_PROJ), BF16) + _nbytes((TM_PROJ, TN_PROJ), BF16))
    vmem += _nbytes((TM_PROJ, d), BF16) + 3 * _nbytes((TM_PROJ, d), F32) // 2 + 2 * _nbytes((TM_PROJ, TN_PROJ), F32)
    kern = functools.partial(_in_proj_kernel, norm_lo=norm_lo, norm_hi=norm_hi)
    return pl.pallas_call(
        kern,
        out_shape=jax.ShapeDtypeStruct((b, s, n), BF16),
        grid=(b, s // TM_PROJ, n // TN_PROJ),
        in_specs=[
            pl.BlockSpec((1, TM_PROJ, d), lambda bi, i, j: (bi, i, 0)),
            pl.BlockSpec((1, 1, 1, d), lambda bi, i, j: (bi, 0, 0, 0)),
            pl.BlockSpec((1, 1, 1, d), lambda bi, i, j: (bi, 1, 0, 0)),
            pl.BlockSpec((1, d), lambda bi, i, j: (0, 0)),
            pl.BlockSpec((d, TN_PROJ), lambda bi, i, j: (0, j)),
            pl.BlockSpec((1, TN_PROJ), lambda bi, i, j: (0, j)),
            pl.BlockSpec((1, TN_PROJ), lambda bi, i, j: (0, j)),
        ],
        out_specs=pl.BlockSpec((1, TM_PROJ, TN_PROJ), lambda bi, i, j: (bi, i, j)),
        scratch_shapes=[pltpu.VMEM((TM_PROJ, d), BF16)],
        compiler_params=_compiler_params(("arbitrary", "arbitrary", "arbitrary"), vmem),
        name="in_proj",
    )(x, mod, mod, ln, w_in, gains, flags)


def _attn_a_kernel(sink_ref, q_ref, kp_ref, km_ref, kn_ref, vp_ref, vm_ref, vn_ref, bias_ref,
                   o_ref):
    hk = pl.program_id(1)
    n = pl.program_id(2)
    last = pl.num_programs(2) - 1
    kcat = jnp.concatenate([kp_ref[0], km_ref[0], kn_ref[0]], axis=0)
    vcat = jnp.concatenate([vp_ref[0], vm_ref[0], vn_ref[0]], axis=0)
    bias = bias_ref[0]
    sink = jnp.concatenate(
        [jnp.full((1, A_BLOCK), sink_ref[hk * A_GROUP + g], F32) for g in range(A_GROUP)], axis=1)
    nt = TQ_A // A_BLOCK
    for t in range(nt):
        q = q_ref[0, t * A_BLOCK:(t + 1) * A_BLOCK, :]
        qs = jnp.concatenate([q[:, g * HEAD_DIM:(g + 1) * HEAD_DIM] for g in range(A_GROUP)], axis=0)
        kw = kcat[t * A_BLOCK:(t + 3) * A_BLOCK]
        vw = vcat[t * A_BLOCK:(t + 3) * A_BLOCK]
        s = lax.dot_general(kw, qs, (((1,), (1,)), ((), ())), preferred_element_type=F32)
        s = s * SCALE + bias
        if t == 0:
            s = jnp.concatenate([jnp.where(n > 0, s[:A_BLOCK], NEG), s[A_BLOCK:]], axis=0)
        if t == nt - 1:
            s = jnp.concatenate([s[:2 * A_BLOCK], jnp.where(n < last, s[2 * A_BLOCK:], NEG)], axis=0)
        m = jnp.maximum(jnp.max(s, axis=0, keepdims=True), sink)
        p = jnp.exp(s - m)
        den = jnp.sum(p, axis=0, keepdims=True) + jnp.exp(sink - m)
        ot = lax.dot_general(vw, p.astype(BF16), (((0,), (0,)), ((), ())),
                             preferred_element_type=F32)
        o = (ot / den).T
        for g in range(A_GROUP):
            o_ref[0, t * A_BLOCK:(t + 1) * A_BLOCK, g * HEAD_DIM:(g + 1) * HEAD_DIM] = (
                o[g * A_BLOCK:(g + 1) * A_BLOCK].astype(BF16))


def _attn_a(z, sink, bias):
    b, s, _ = z.shape
    nblk = s // A_BLOCK
    per = TQ_A // A_BLOCK
    qw = A_GROUP * HEAD_DIM
    q0, k0, v0 = Z_QA // qw, Z_KA // HEAD_DIM, Z_VA // HEAD_DIM

    def prev_map(col0):
        return lambda bi, h, n: (bi, jnp.maximum(n * per - 1, 0), col0 + h)

    def main_map(col0):
        return lambda bi, h, n: (bi, n, col0 + h)

    def next_map(col0):
        return lambda bi, h, n: (bi, jnp.minimum((n + 1) * per, nblk - 1), col0 + h)

    edge = (1, A_BLOCK, HEAD_DIM)
    main = (1, TQ_A, HEAD_DIM)
    vmem = 2 * (2 * _nbytes((TQ_A, qw), BF16) + 2 * _nbytes((TQ_A + 2 * A_BLOCK, HEAD_DIM), BF16)
                + _nbytes((qw, 3 * A_BLOCK), F32)) + 8 * _nbytes((qw, 3 * A_BLOCK), F32) + (4 << 20)
    return pl.pallas_call(
        _attn_a_kernel,
        out_shape=jax.ShapeDtypeStruct((b, s, A_HEADS * HEAD_DIM), BF16),
        grid=(b, A_KV_HEADS, s // TQ_A),
        in_specs=[
            pl.BlockSpec(memory_space=pltpu.SMEM),
            pl.BlockSpec((1, TQ_A, qw), lambda bi, h, n: (bi, n, q0 + h)),
            pl.BlockSpec(edge, prev_map(k0)),
            pl.BlockSpec(main, main_map(k0)),
            pl.BlockSpec(edge, next_map(k0)),
            pl.BlockSpec(edge, prev_map(v0)),
            pl.BlockSpec(main, main_map(v0)),
            pl.BlockSpec(edge, next_map(v0)),
            pl.BlockSpec((1, 3 * A_BLOCK, qw), lambda bi, h, n: (h, 0, 0)),
        ],
        out_specs=pl.BlockSpec((1, TQ_A, qw), lambda bi, h, n: (bi, n, h)),
        compiler_params=_compiler_params(("arbitrary", "arbitrary", "arbitrary"), vmem),
        name="attn_a",
    )(sink, z, z, z, z, z, z, z, bias)


def _alibi_window_bias():
    i = np.arange(A_BLOCK)[None, :]
    j = np.arange(3 * A_BLOCK)[:, None]
    rel = np.abs(A_BLOCK + i - j).astype(np.float32)
    slopes = 2.0 ** (-8.0 * np.arange(1, A_HEADS + 1, dtype=np.float32) / A_HEADS)
    bias = np.where(rel[None] <= A_BLOCK, -slopes[:, None, None] * rel[None], np.float32(NEG))
    bias = bias.astype(np.float32).reshape(A_KV_HEADS, A_GROUP, 3 * A_BLOCK, A_BLOCK)
    bias = bias.transpose(0, 2, 1, 3).reshape(A_KV_HEADS, 3 * A_BLOCK, A_GROUP * A_BLOCK)
    return jnp.asarray(bias)


def _attn_b_kernel(q_ref, k_ref, v_ref, bias_ref, o_ref, *, rows, unroll):
    kh = NB_ROWS
    span = (kh + 1) * GRID_W

    def one_pair(i):
        r = 2 * i
        start = jnp.minimum(jnp.clip(r - kh // 2, 0, rows - kh), rows - kh - 1)
        qs = pl.multiple_of(r * GRID_W, 2 * GRID_W)
        ks = pl.multiple_of(start * GRID_W, GRID_W)
        q = q_ref[0, pl.ds(qs, 2 * GRID_W), :]
        kw = k_ref[0, pl.ds(ks, span), :]
        vw = v_ref[0, pl.ds(ks, span), :]
        s = lax.dot_general(kw, q, (((1,), (1,)), ((), ())), preferred_element_type=F32)
        s = s * SCALE + bias_ref[0, r - start]
        m = jnp.max(s, axis=0, keepdims=True)
        p = jnp.exp(s - m)
        den = jnp.sum(p, axis=0, keepdims=True)
        ot = lax.dot_general(vw, p.astype(BF16), (((0,), (0,)), ((), ())),
                             preferred_element_type=F32)
        o_ref[0, pl.ds(qs, 2 * GRID_W), :] = (ot / den).T.astype(BF16)

    def body(i, carry):
        for u in range(unroll):
            one_pair(i * unroll + u)
        return carry

    lax.fori_loop(0, rows // (2 * unroll), body, 0)


def _attn_b(z, bias):
    b, s, _ = z.shape
    rows = s // GRID_W
    unroll = 4
    assert rows > NB_ROWS and rows % (2 * unroll) == 0
    q0, k0, v0 = Z_QB // HEAD_DIM, Z_KB // HEAD_DIM, Z_VB // HEAD_DIM
    blk = (1, s, HEAD_DIM)
    vmem = 2 * (4 * _nbytes((s, HEAD_DIM), BF16) + _nbytes(bias.shape[1:], F32)) + (8 << 20)
    kern = functools.partial(_attn_b_kernel, rows=rows, unroll=unroll)
    return pl.pallas_call(
        kern,
        out_shape=jax.ShapeDtypeStruct((b, s, B_HEADS * HEAD_DIM), BF16),
        grid=(B_HEADS, b),
        in_specs=[
            pl.BlockSpec(blk, lambda h, bi: (bi, 0, q0 + h)),
            pl.BlockSpec(blk, lambda h, bi: (bi, 0, k0 + h)),
            pl.BlockSpec(blk, lambda h, bi: (bi, 0, v0 + h)),
            pl.BlockSpec((1,) + bias.shape[1:], lambda h, bi: (h, 0, 0, 0)),
        ],
        out_specs=pl.BlockSpec(blk, lambda h, bi: (bi, 0, h)),
        compiler_params=_compiler_params(("arbitrary", "arbitrary"), vmem),
        name="attn_b",
    )(z, z, z, bias)


def _neighbourhood_bias(rpb):
    kh = NB_ROWS
    c = np.arange(GRID_W)
    cw = np.clip(c - NB_COLS // 2, 0, GRID_W - NB_COLS)
    col_ok = (c[None, :] >= cw[:, None]) & (c[None, :] < cw[:, None] + NB_COLS)
    dc = np.clip(c[None, :] - c[:, None], -(NB_COLS - 1), NB_COLS - 1) + (NB_COLS - 1)
    onehot = (dc[None] == np.arange(2 * NB_COLS - 1)[:, None, None]).astype(np.float32)
    cols = jnp.sum(rpb.astype(F32)[:, :, :, None, None] * onehot[None, None], axis=2)
    crow = np.arange(kh)[:, None, None] + np.arange(2)[None, :, None]
    j = np.arange(kh + 1)[None, None, :]
    off = (crow > kh // 2).astype(np.int64)
    row_ok = (j >= off) & (j < off + kh)
    colsp = jnp.pad(cols, ((0, 0), (2, 2), (0, 0), (0, 0)))
    t = jnp.stack([colsp[:, kh + 1 - int(cc):2 * kh + 2 - int(cc)] for cc in crow.reshape(-1)], axis=1)
    t = t.reshape(rpb.shape[0], kh, 2, kh + 1, GRID_W, GRID_W)
    ok = row_ok[:, :, :, None, None] & col_ok[None, None, None, :, :]
    t = jnp.where(ok[None], t, NEG)
    t = jnp.transpose(t, (0, 1, 3, 5, 2, 4))
    return t.reshape(rpb.shape[0], kh, (kh + 1) * GRID_W, 2 * GRID_W)


def _mix_kernel(oa_ref, ob_ref, ga_ref, gb_ref, wa_ref, wb_ref, m_ref):
    pa = jnp.dot(oa_ref[0], wa_ref[...], preferred_element_type=F32)
    pb = jnp.dot(ob_ref[0], wb_ref[...], preferred_element_type=F32)
    ga = _sigmoid(ga_ref[0].astype(F32))
    gb = _sigmoid(gb_ref[0].astype(F32))
    m_ref[0] = (ga * pa + gb * pb).astype(BF16)


def _mix(oa, ob, z, w_br_a, w_br_b):
    b, s, wa = oa.shape
    d = w_br_a.shape[1]
    assert Z_GA % d == 0 and Z_GB % d == 0
    tm = TM_MIX
    vmem = 2 * (2 * _nbytes((tm, wa), BF16) + 3 * _nbytes((tm, d), BF16) + 2 * _nbytes((wa, d), BF16))
    vmem += 5 * _nbytes((tm, d), F32)
    return pl.pallas_call(
        _mix_kernel,
        out_shape=jax.ShapeDtypeStruct((b, s, d), BF16),
        grid=(b, s // tm),
        in_specs=[
            pl.BlockSpec((1, tm, wa), lambda bi, i: (bi, i, 0)),
            pl.BlockSpec((1, tm, wa), lambda bi, i: (bi, i, 0)),
            pl.BlockSpec((1, tm, d), lambda bi, i: (bi, i, Z_GA // d)),
            pl.BlockSpec((1, tm, d), lambda bi, i: (bi, i, Z_GB // d)),
            pl.BlockSpec((wa, d), lambda bi, i: (0, 0)),
            pl.BlockSpec((wa, d), lambda bi, i: (0, 0)),
        ],
        out_specs=pl.BlockSpec((1, tm, d), lambda bi, i: (bi, i, 0)),
        compiler_params=_compiler_params(("arbitrary", "arbitrary"), vmem),
        name="mix",
    )(oa, ob, z, z, w_br_a, w_br_b)


def _out_proj_kernel(m_ref, x_ref, g_ref, sh_ref, sc_ref, ln_ref, w_ref, o_ref, h_ref):
    y = jnp.dot(m_ref[0], w_ref[...], preferred_element_type=F32)
    x1 = x_ref[0] + g_ref[0, 0] * y
    o_ref[0] = x1
    h_ref[0] = _modulated_rmsnorm(x1, ln_ref[...], sh_ref[0, 0], sc_ref[0, 0]).astype(BF16)


def _out_proj(m, x, mod, ln2, w_out):
    b, s, d = x.shape
    tm = TM_MIX
    vmem = 2 * (2 * _nbytes((tm, d), BF16) + 2 * _nbytes((tm, d), F32) + _nbytes((d, d), BF16))
    vmem += 4 * _nbytes((tm, d), F32)
    row = pl.BlockSpec((1, tm, d), lambda bi, i: (bi, i, 0))
    return pl.pallas_call(
        _out_proj_kernel,
        out_shape=(jax.ShapeDtypeStruct((b, s, d), F32), jax.ShapeDtypeStruct((b, s, d), BF16)),
        grid=(b, s // tm),
        in_specs=[
            row,
            row,
            pl.BlockSpec((1, 1, 1, d), lambda bi, i: (bi, 2, 0, 0)),
            pl.BlockSpec((1, 1, 1, d), lambda bi, i: (bi, 3, 0, 0)),
            pl.BlockSpec((1, 1, 1, d), lambda bi, i: (bi, 4, 0, 0)),
            pl.BlockSpec((1, d), lambda bi, i: (0, 0)),
            pl.BlockSpec((d, d), lambda bi, i: (0, 0)),
        ],
        out_specs=(row, row),
        compiler_params=_compiler_params(("arbitrary", "arbitrary"), vmem),
        name="out_proj",
    )(m, x, mod, mod, mod, ln2, w_out)


def _mlp_kernel(x_ref, h_ref, g_ref, w1_ref, w2_ref, o_ref):
    f = pl.program_id(2)

    @pl.when(f == 0)
    def _():
        o_ref[...] = jnp.zeros_like(o_ref)

    u = jnp.dot(h_ref[0], w1_ref[...], preferred_element_type=F32)
    a = jnp.square(jnp.maximum(u, 0.0)).astype(BF16)
    o_ref[0] += jnp.dot(a, w2_ref[...], preferred_element_type=F32)

    @pl.when(f == pl.num_programs(2) - 1)
    def _():
        o_ref[0] = x_ref[0] + g_ref[0, 0] * o_ref[0]


def _mlp(x, h2, mod, w1, w2):
    b, s, d = x.shape
    ff = w1.shape[1]
    tm, tf = TM_MLP, TF_MLP
    vmem = 2 * (2 * _nbytes((tm, d), F32) + _nbytes((tm, d), BF16) + 2 * _nbytes((d, tf), BF16))
    vmem += 2 * _nbytes((tm, tf), F32) + 2 * _nbytes((tm, d), F32)
    return pl.pallas_call(
        _mlp_kernel,
        out_shape=jax.ShapeDtypeStruct((b, s, d), F32),
        grid=(b, s // tm, ff // tf),
        in_specs=[
            pl.BlockSpec((1, tm, d), lambda bi, i, f: (bi, i, 0)),
            pl.BlockSpec((1, tm, d), lambda bi, i, f: (bi, i, 0)),
            pl.BlockSpec((1, 1, 1, d), lambda bi, i, f: (bi, 5, 0, 0)),
            pl.BlockSpec((d, tf), lambda bi, i, f: (0, f)),
            pl.BlockSpec((tf, d), lambda bi, i, f: (f, 0)),
        ],
        out_specs=pl.BlockSpec((1, tm, d), lambda bi, i, f: (bi, i, 0)),
        compiler_params=_compiler_params(("arbitrary", "arbitrary", "arbitrary"), vmem),
        name="mlp",
    )(x, h2, mod, w1, w2)


def _split_w_in(w_in_l):
    aq, akv, bw = A_HEADS * HEAD_DIM, A_KV_HEADS * HEAD_DIM, B_HEADS * HEAD_DIM
    cuts = np.cumsum([aq, akv, akv, bw, bw, bw, D_MODEL])
    qa, ka, va, qb, kb, vb, ga, gb = jnp.split(w_in_l, cuts, axis=1)
    return jnp.concatenate([ga, gb, qa, ka, va, qb, kb, vb], axis=1).astype(BF16)


def _qk_norm_vectors(qn_a, kn_a, qn_b, kn_b):
    ones = jnp.ones((HEAD_DIM,), F32)
    chunks, flags = [], []
    for gain, count, on in (
        (ones, 2 * D_MODEL // HEAD_DIM, 0.0),
        (qn_a, A_HEADS, 1.0), (kn_a, A_KV_HEADS, 1.0), (ones, A_KV_HEADS, 0.0),
        (qn_b, B_HEADS, 1.0), (kn_b, B_HEADS, 1.0), (ones, B_HEADS, 0.0),
    ):
        chunks.append(jnp.tile(gain.astype(F32), count))
        flags.append(jnp.full((count * HEAD_DIM,), on, F32))
    return jnp.concatenate(chunks)[None, :], jnp.concatenate(flags)[None, :]


def kernel(x_prompt, x_sample, c_prompt, c_sample, w_ada, b_ada, ln1, ln2, w_in, qn_a, kn_a, qn_b,
           kn_b, sink_a, rpb_b, w_br_a, w_br_b, w_out, w_mlp1, w_mlp2):
    depth = w_ada.shape[0]
    nb_p = x_prompt.shape[0]
    c = jnp.concatenate([c_prompt, c_sample], axis=0)
    mod = _adaln(c, w_ada, b_ada)
    mod = mod.reshape(depth, c.shape[0], 6, 1, D_MODEL)
    alibi = _alibi_window_bias()
    xs = [x_prompt, x_sample]
    for l in range(depth):
        w_in_l = _split_w_in(w_in[l])
        gains, flags = _qk_norm_vectors(qn_a[l], kn_a[l], qn_b[l], kn_b[l])
        nbias = _neighbourhood_bias(rpb_b[l])
        wa, wb = w_br_a[l].astype(BF16), w_br_b[l].astype(BF16)
        wo, w1, w2 = w_out[l].astype(BF16), w_mlp1[l].astype(BF16), w_mlp2[l].astype(BF16)
        ln1_l, ln2_l = ln1[l][None, :], ln2[l][None, :]
        sink_l = sink_a[l].astype(F32)
        for g in range(2):
            x = xs[g]
            mod_g = mod[l, :nb_p] if g == 0 else mod[l, nb_p:]
            z = _in_proj(x, mod_g, ln1_l, w_in_l, gains, flags)
            oa = _attn_a(z, sink_l, alibi)
            ob = _attn_b(z, nbias)
            m = _mix(oa, ob, z, wa, wb)
            x, h2 = _out_proj(m, x, mod_g, ln2_l, wo)
            xs[g] = _mlp(x, h2, mod_g, w1, w2)
    return (xs[0], xs[1])
```
